```python
import math
import jax, jax.numpy as jnp
from jax import lax
import numpy as np

D_MODEL = 2048
BATCH = 2
SEQ = 8192
DEPTH = 2
DEC_BATCH = 32
DEC_SEQ = 32
PAST_LEN = 4096

CHUNK = 64
N_A_LAYERS = DEPTH // 2
N_B_LAYERS = DEPTH - N_A_LAYERS
SSM_GROUP = 16
N_GROUPS = D_MODEL // SSM_GROUP
SSM_STATE = 64
SSM_BLOCK = 128
STEP_MIN = 1e-3
STEP_MAX = 1e-1
N_HEADS = 16
HEAD_DIM = D_MODEL // N_HEADS // 2
V_DIM = 2 * HEAD_DIM
Q_BLOCK = 128
D_FF = ((8 * D_MODEL + 3 * 256 - 1) // (3 * 256)) * 256
ROPE_THETA = 10000.0
EPS = 1e-6

kernel_name = 'yoco_s5_diffattn_stream_step'


def rmsnorm(x, g):
    x32 = x.astype(jnp.float32)
    y = x32 * lax.rsqrt(jnp.mean(x32 * x32, axis=-1, keepdims=True) + EPS)
    return (y * g.astype(jnp.float32)).astype(x.dtype)


def rope(x, pos):
    half = x.shape[-1] // 2
    inv = ROPE_THETA ** (-jnp.arange(half, dtype=jnp.float32) / half)
    ang = pos.astype(jnp.float32)[:, None] * inv[None, :]
    shape = (1, ang.shape[0]) + (1,) * (x.ndim - 3) + (half,)
    cos = jnp.cos(ang).reshape(shape)
    sin = jnp.sin(ang).reshape(shape)
    x32 = x.astype(jnp.float32)
    x1, x2 = x32[..., :half], x32[..., half:]
    return jnp.concatenate([x1 * cos - x2 * sin, x2 * cos + x1 * sin], axis=-1).astype(x.dtype)


def swiglu(x, g, w_in, w_out):
    h = rmsnorm(x, g) @ w_in
    a, b = jnp.split(h, 2, axis=-1)
    return (jax.nn.silu(a) * b) @ w_out


def s5_discretise(lam_re, lam_im, log_step, b_re, b_im, c_re, c_im):
    f32 = jnp.float32
    lam = lax.complex(lam_re.astype(f32), lam_im.astype(f32))
    dt = jnp.exp(log_step.astype(f32))[:, None]
    a_bar = jnp.exp(lam * dt)
    b = lax.complex(b_re.astype(f32), b_im.astype(f32))
    b_bar = ((a_bar - 1.0) / lam)[..., None] * b
    c = lax.complex(c_re.astype(f32), c_im.astype(f32))
    return a_bar, b_bar, c


def _combine(e1, e2):
    a1, b1 = e1
    a2, b2 = e2
    return a1 * a2, a2 * b1 + b2


def s5_scan_block(h0, u, a_bar, b_bar, c, d):
    bu = jnp.einsum('blgp,gnp->blgn', u.astype(jnp.complex64), b_bar)
    a = jnp.broadcast_to(a_bar, bu.shape)
    a_cum, h = lax.associative_scan(_combine, (a, bu), axis=1)
    h = h + a_cum * h0[:, None]
    y = jnp.einsum('blgn,gpn->blgp', h, c).real + d * u
    return h[:, -1], y


def s5_mixer(x, h0, g, lam_re, lam_im, log_step, b_re, b_im, c_re, c_im, d, w_glu, blocked):
    bsz, slen, _ = x.shape
    u = rmsnorm(x, g).astype(jnp.float32).reshape(bsz, slen, N_GROUPS, SSM_GROUP)
    a_bar, b_bar, c = s5_discretise(lam_re, lam_im, log_step, b_re, b_im, c_re, c_im)
    dd = d.astype(jnp.float32).reshape(N_GROUPS, SSM_GROUP)
    if blocked:
        nb = slen // SSM_BLOCK
        ub = jnp.moveaxis(u.reshape(bsz, nb, SSM_BLOCK, N_GROUPS, SSM_GROUP), 1, 0)

        def step(h, u_blk):
            return s5_scan_block(h, u_blk, a_bar, b_bar, c, dd)

        h_last, yb = lax.scan(step, h0, ub)
        y = jnp.moveaxis(yb, 0, 1).reshape(bsz, slen, D_MODEL)
    else:
        h_last, y = s5_scan_block(h0, u, a_bar, b_bar, c, dd)
        y = y.reshape(bsz, slen, D_MODEL)
    z = jax.nn.gelu(y).astype(x.dtype) @ w_glu
    zv, zg = jnp.split(z, 2, axis=-1)
    return zv * jax.nn.sigmoid(zg), h_last


def shared_kv(x, pos, g_kv, w_kv, k_norm):
    bsz, slen, _ = x.shape
    h = rmsnorm(x, g_kv) @ w_kv
    k = h[..., :2 * N_HEADS * HEAD_DIM].reshape(bsz, slen, N_HEADS, 2, HEAD_DIM)
    v = h[..., 2 * N_HEADS * HEAD_DIM:].reshape(bsz, slen, N_HEADS, V_DIM)
    k = rope(rmsnorm(k, k_norm), pos)
    return k, v


def diff_attend(q, k, v, q_pos, k_pos, lam):
    s = jnp.einsum('bqhtd,bkhtd->bhtqk', q, k).astype(jnp.float32) * (HEAD_DIM ** -0.5)
    mask = (k_pos[None, :] // CHUNK) <= (q_pos[:, None] // CHUNK)
    s = jnp.where(mask, s, -jnp.inf)
    p = jax.nn.softmax(s, axis=-1)
    a = p[:, :, 0] - lam * p[:, :, 1]
    return jnp.einsum('bhqk,bkhe->bqhe', a.astype(v.dtype), v)


def diff_attn_mixer(x, pos, k, v, k_pos, g, w_q, q_norm, lq1, lk1, lq2, lk2, subln, w_o,
                    lambda_init, blocked):
    bsz, slen, _ = x.shape
    q = (rmsnorm(x, g) @ w_q).reshape(bsz, slen, N_HEADS, 2, HEAD_DIM)
    q = rope(rmsnorm(q, q_norm), pos)
    f32 = jnp.float32
    lam = (jnp.exp(jnp.sum(lq1.astype(f32) * lk1.astype(f32)))
           - jnp.exp(jnp.sum(lq2.astype(f32) * lk2.astype(f32))) + lambda_init)
    if blocked:
        nb = slen // Q_BLOCK
        qb = jnp.moveaxis(q.reshape(bsz, nb, Q_BLOCK, N_HEADS, 2, HEAD_DIM), 1, 0)
        pb = pos.reshape(nb, Q_BLOCK)
        ob = lax.map(lambda a: diff_attend(a[0], k, v, a[1], k_pos, lam), (qb, pb))
        o = jnp.moveaxis(ob, 0, 1).reshape(bsz, slen, N_HEADS, V_DIM)
    else:
        o = diff_attend(q, k, v, pos, k_pos, lam)
    o = rmsnorm(o, subln) * (1.0 - lambda_init)
    return o.reshape(bsz, slen, N_HEADS * V_DIM) @ w_o


def trunk(x, pos, ssm_h0, cache_k, cache_v, p, prompt):
    new_h = []
    k_new = v_new = k_all = v_all = k_pos = None
    for layer in range(DEPTH):
        if layer < N_A_LAYERS:
            i = layer
            out, h_last = s5_mixer(x, ssm_h0[i], p['norm_ssm'][i], p['ssm_lam_re'][i], p['ssm_lam_im'][i],
                                   p['ssm_log_step'][i], p['ssm_b_re'][i], p['ssm_b_im'][i],
                                   p['ssm_c_re'][i], p['ssm_c_im'][i], p['ssm_d'][i], p['w_glu'][i],
                                   prompt)
            x = x + out
            new_h.append(h_last)
        else:
            if layer == N_A_LAYERS:
                k_new, v_new = shared_kv(x, pos, p['norm_kv'], p['w_kv'], p['k_norm'])
                if prompt:
                    k_all, v_all, k_pos = k_new, v_new, pos
                else:
                    k_all = jnp.concatenate([cache_k.astype(k_new.dtype), k_new], axis=1)
                    v_all = jnp.concatenate([cache_v.astype(v_new.dtype), v_new], axis=1)
                    k_pos = jnp.arange(PAST_LEN + x.shape[1], dtype=jnp.int32)
            j = layer - N_A_LAYERS
            lambda_init = 0.8 - 0.6 * math.exp(-0.3 * layer)
            x = x + diff_attn_mixer(x, pos, k_all, v_all, k_pos, p['norm_attn'][j], p['w_q'][j],
                                    p['q_norm'][j], p['lambda_q1'][j], p['lambda_k1'][j],
                                    p['lambda_q2'][j], p['lambda_k2'][j], p['subln'][j], p['w_o'][j],
                                    lambda_init, prompt)
        x = x + swiglu(x, p['norm_ffn'][layer], p['w_ffn_in'][layer], p['w_ffn_out'][layer])
    h = jnp.stack(new_h)
    return x, h, k_new, v_new


def setup_inputs(seed: int = 0) -> dict:
    key = jax.random.key(seed)
    ks = iter(jax.random.split(key, 40))

    def nrm(shape, scale):
        return jax.random.normal(next(ks), shape, jnp.float32) * scale

    hd, H, D, G, N, P = HEAD_DIM, N_HEADS, D_MODEL, N_GROUPS, SSM_STATE, SSM_GROUP
    out = {}
    out['x_prompt'] = nrm((BATCH, SEQ, D), 1.0)
    out['x_sample'] = nrm((DEC_BATCH, DEC_SEQ, D), 1.0)
    out['state_ssm_re'] = nrm((N_A_LAYERS, DEC_BATCH, G, N), 0.1)
    out['state_ssm_im'] = nrm((N_A_LAYERS, DEC_BATCH, G, N), 0.1)
    out['cache_k'] = nrm((DEC_BATCH, PAST_LEN, H, 2, hd), 1.0)
    out['cache_v'] = nrm((DEC_BATCH, PAST_LEN, H, V_DIM), 1.0)
    out['norm_ssm'] = 1.0 + nrm((N_A_LAYERS, D), 0.02)
    out['ssm_lam_re'] = -0.5 + nrm((N_A_LAYERS, G, N), 0.01)
    out['ssm_lam_im'] = math.pi * jnp.arange(N, dtype=jnp.float32) + nrm((N_A_LAYERS, G, N), 0.01)
    out['ssm_log_step'] = jax.random.uniform(next(ks), (N_A_LAYERS, G), jnp.float32,
                                             minval=math.log(STEP_MIN), maxval=math.log(STEP_MAX))
    out['ssm_b_re'] = nrm((N_A_LAYERS, G, N, P), (2 * P) ** -0.5)
    out['ssm_b_im'] = nrm((N_A_LAYERS, G, N, P), (2 * P) ** -0.5)
    out['ssm_c_re'] = nrm((N_A_LAYERS, G, P, N), N ** -0.5)
    out['ssm_c_im'] = nrm((N_A_LAYERS, G, P, N), N ** -0.5)
    out['ssm_d'] = nrm((N_A_LAYERS, D), 1.0)
    out['w_glu'] = nrm((N_A_LAYERS, D, 2 * D), D ** -0.5)
    out['norm_kv'] = 1.0 + nrm((D,), 0.02)
    out['w_kv'] = nrm((D, 2 * H * hd + H * V_DIM), D ** -0.5)
    out['k_norm'] = 1.0 + nrm((hd,), 0.02)
    out['norm_attn'] = 1.0 + nrm((N_B_LAYERS, D), 0.02)
    out['w_q'] = nrm((N_B_LAYERS, D, 2 * H * hd), D ** -0.5)
    out['q_norm'] = 1.0 + nrm((N_B_LAYERS, hd), 0.02)
    out['lambda_q1'] = nrm((N_B_LAYERS, hd), 0.1)
    out['lambda_k1'] = nrm((N_B_LAYERS, hd), 0.1)
    out['lambda_q2'] = nrm((N_B_LAYERS, hd), 0.1)
    out['lambda_k2'] = nrm((N_B_LAYERS, hd), 0.1)
    out['subln'] = 1.0 + nrm((N_B_LAYERS, V_DIM), 0.02)
    out['w_o'] = nrm((N_B_LAYERS, H * V_DIM, D), (H * V_DIM) ** -0.5)
    out['norm_ffn'] = 1.0 + nrm((DEPTH, D), 0.02)
    out['w_ffn_in'] = nrm((DEPTH, D, 2 * D_FF), D ** -0.5)
    out['w_ffn_out'] = nrm((DEPTH, D_FF, D), D_FF ** -0.5)
    return out


def reference(x_prompt, x_sample, state_ssm_re, state_ssm_im, cache_k, cache_v,
              norm_ssm, ssm_lam_re, ssm_lam_im, ssm_log_step, ssm_b_re, ssm_b_im, ssm_c_re, ssm_c_im,
              ssm_d, w_glu, norm_kv, w_kv, k_norm, norm_attn, w_q, q_norm,
              lambda_q1, lambda_k1, lambda_q2, lambda_k2, subln, w_o,
              norm_ffn, w_ffn_in, w_ffn_out):
    p = dict(norm_ssm=norm_ssm, ssm_lam_re=ssm_lam_re, ssm_lam_im=ssm_lam_im, ssm_log_step=ssm_log_step,
             ssm_b_re=ssm_b_re, ssm_b_im=ssm_b_im, ssm_c_re=ssm_c_re, ssm_c_im=ssm_c_im, ssm_d=ssm_d,
             w_glu=w_glu, norm_kv=norm_kv, w_kv=w_kv, k_norm=k_norm, norm_attn=norm_attn, w_q=w_q,
             q_norm=q_norm, lambda_q1=lambda_q1, lambda_k1=lambda_k1, lambda_q2=lambda_q2,
             lambda_k2=lambda_k2, subln=subln, w_o=w_o, norm_ffn=norm_ffn, w_ffn_in=w_ffn_in,
             w_ffn_out=w_ffn_out)

    bp, sp = x_prompt.shape[0], x_prompt.shape[1]
    pos_p = jnp.arange(sp, dtype=jnp.int32)
    h0_p = jnp.zeros((N_A_LAYERS, bp, N_GROUPS, SSM_STATE), jnp.complex64)
    y_prompt, h_p, k_p, v_p = trunk(x_prompt, pos_p, h0_p, None, None, p, True)

    sd = x_sample.shape[1]
    pos_s = PAST_LEN + jnp.arange(sd, dtype=jnp.int32)
    h0_s = lax.complex(state_ssm_re.astype(jnp.float32), state_ssm_im.astype(jnp.float32))
    y_sample, h_s, k_s, v_s = trunk(x_sample, pos_s, h0_s, cache_k, cache_v, p, False)

    return (y_prompt, y_sample, h_p.real, h_p.imag, k_p, v_p, h_s.real, h_s.imag, k_s, v_s)
```

```python
import functools
import math

import jax
import jax.numpy as jnp
from jax import lax
from jax.experimental import pallas as pl
from jax.experimental.pallas import tpu as pltpu

CHUNK = 64
CHUNK_SHIFT = CHUNK.bit_length() - 1
EPS = 1e-6
ROPE_THETA = 10000.0
LANES = 128
SUBLANES = 8
SSM_SEG = SUBLANES
VMEM_LIMIT = 56 * 1024 * 1024

F32 = jnp.float32
BF16 = jnp.bfloat16
NEG = -1e30


def _dot(a, b):
    return jnp.dot(a, b, preferred_element_type=F32)


def _dot_nt(a, b):
    return lax.dot_general(a, b, (((1,), (1,)), ((), ())), preferred_element_type=F32)


def _rms(x, g):
    return x * lax.rsqrt(jnp.mean(x * x, axis=-1, keepdims=True) + EPS) * g


def _split_bf16(x):
    hi = x.astype(BF16)
    lo = (x - hi.astype(F32)).astype(BF16)
    return hi, lo


def _params(*sem):
    return pltpu.CompilerParams(dimension_semantics=sem, vmem_limit_bytes=VMEM_LIMIT)


def _cmul(ar, ai, br, bi):
    return ar * br - ai * bi, ar * bi + ai * br


def _cpow(ar, ai, n):
    rr, ri = None, None
    br, bi = ar, ai
    while n:
        if n & 1:
            rr, ri = (br, bi) if rr is None else _cmul(rr, ri, br, bi)
        n >>= 1
        if n:
            br, bi = _cmul(br, bi, br, bi)
    return rr, ri


def _s5_kernel(*refs, chain, seg_len, chunks_per_seq, n_cb, cw):
    if chain:
        (x_ref, g_ref, p_ref, pt_ref, bhi_ref, blo_ref, c_ref, are_ref, aim_ref, d_ref,
         gy_ref, hre_ref, him_ref, uhi_scr, ulo_scr, bu_scr, gyp_scr, cre_scr, cim_scr) = refs
    else:
        (x_ref, g_ref, p_ref, pt_ref, bhi_ref, blo_ref, c_ref, are_ref, aim_ref, d_ref,
         h0re_ref, h0im_ref, gy_ref, hre_ref, him_ref, uhi_scr, ulo_scr, bu_scr, gyp_scr) = refs
    sw = cw // 2

    u = _rms(x_ref[...], g_ref[...])
    u_hi, u_lo = _split_bf16(u)
    perm = p_ref[...]
    up_hi = _dot(perm, u_hi).astype(BF16)
    up_lo = _dot(perm, u_lo).astype(BF16)
    for cb in range(n_cb):
        uhi_scr[cb] = up_hi[:, cb * LANES:(cb + 1) * LANES]
        ulo_scr[cb] = up_lo[:, cb * LANES:(cb + 1) * LANES]

    if chain:
        @pl.when(pl.program_id(0) % chunks_per_seq == 0)
        def _():
            cre_scr[...] = jnp.zeros_like(cre_scr)
            cim_scr[...] = jnp.zeros_like(cim_scr)

    row_id = lax.broadcasted_iota(jnp.int32, (SSM_SEG, sw), 0)

    def colblock(cb, carry):
        uh = uhi_scr[cb]
        ul = ulo_scr[cb]
        bh = bhi_ref[cb]
        bu_scr[...] = _dot(uh, bh) + _dot(ul, bh) + _dot(uh, blo_ref[cb])
        are = are_ref[cb]
        aim = aim_ref[cb]
        if chain:
            hre = jnp.zeros((SSM_SEG, sw), F32)
            him = jnp.zeros((SSM_SEG, sw), F32)
        else:
            hre = h0re_ref[cb]
            him = h0im_ref[cb]
        for k in range(seg_len):
            r0 = k * SSM_SEG
            nre = are * hre - aim * him + bu_scr[r0:r0 + SSM_SEG, 0:sw]
            nim = are * him + aim * hre + bu_scr[r0:r0 + SSM_SEG, sw:cw]
            hre, him = nre, nim
            bu_scr[r0:r0 + SSM_SEG, 0:sw] = hre
            bu_scr[r0:r0 + SSM_SEG, sw:cw] = him

        if chain:
            pre, pim = _cpow(are[0:1], aim[0:1], seg_len)
            cur_re = cre_scr[cb][SSM_SEG - 1:SSM_SEG]
            cur_im = cim_scr[cb][SSM_SEG - 1:SSM_SEG]
            prev_re = jnp.zeros((SSM_SEG, sw), F32)
            prev_im = jnp.zeros((SSM_SEG, sw), F32)
            end_re = jnp.zeros((SSM_SEG, sw), F32)
            end_im = jnp.zeros((SSM_SEG, sw), F32)
            for j in range(SSM_SEG):
                prev_re = jnp.where(row_id == j, cur_re, prev_re)
                prev_im = jnp.where(row_id == j, cur_im, prev_im)
                mre, mim = _cmul(pre, pim, cur_re, cur_im)
                cur_re = mre + hre[j:j + 1]
                cur_im = mim + him[j:j + 1]
                end_re = jnp.where(row_id == j, cur_re, end_re)
                end_im = jnp.where(row_id == j, cur_im, end_im)
            cre_scr[cb] = end_re
            cim_scr[cb] = end_im
            hre_ref[cb] = end_re
            him_ref[cb] = end_im
            qre, qim = _cmul(are, aim, prev_re, prev_im)
            for k in range(seg_len):
                r0 = k * SSM_SEG
                bu_scr[r0:r0 + SSM_SEG, 0:sw] = bu_scr[r0:r0 + SSM_SEG, 0:sw] + qre
                bu_scr[r0:r0 + SSM_SEG, sw:cw] = bu_scr[r0:r0 + SSM_SEG, sw:cw] + qim
                if k + 1 < seg_len:
                    qre, qim = _cmul(are, aim, qre, qim)
        else:
            hre_ref[cb] = hre
            him_ref[cb] = him

        y = _dot(bu_scr[...].astype(BF16), c_ref[cb])
        y = y + d_ref[cb] * (uh.astype(F32) + ul.astype(F32))
        gyp_scr[cb] = jax.nn.gelu(y, approximate=True).astype(BF16)
        return carry

    lax.fori_loop(0, n_cb, colblock, 0)

    perm_t = pt_ref[...]
    for cb in range(n_cb):
        gy_ref[:, cb * LANES:(cb + 1) * LANES] = _dot(perm_t, gyp_scr[cb]).astype(BF16)


def _s5_call(x, g, ssm, h0, *, seg_len, chunks_per_seq):
    m_rows, d_model = x.shape
    chain = h0 is None
    t_rows = SSM_SEG * seg_len
    n_chunks = m_rows // t_rows
    n_cb, _, cw = ssm['b_hi'].shape
    sw = cw // 2
    full3 = lambda i: (0, 0, 0)
    full2 = lambda i: (0, 0)
    in_specs = [
        pl.BlockSpec((t_rows, d_model), lambda i: (i, 0)),
        pl.BlockSpec((1, d_model), full2),
        pl.BlockSpec((t_rows, t_rows), full2),
        pl.BlockSpec((t_rows, t_rows), full2),
        pl.BlockSpec((n_cb, LANES, cw), full3),
        pl.BlockSpec((n_cb, LANES, cw), full3),
        pl.BlockSpec((n_cb, cw, LANES), full3),
        pl.BlockSpec((n_cb, SSM_SEG, sw), full3),
        pl.BlockSpec((n_cb, SSM_SEG, sw), full3),
        pl.BlockSpec((n_cb, 1, LANES), full3),
    ]
    args = [x, g, ssm['perm'], ssm['perm_t'], ssm['b_hi'], ssm['b_lo'], ssm['c'], ssm['a_re'], ssm['a_im'],
            ssm['d']]
    if not chain:
        in_specs += [pl.BlockSpec((n_cb, SSM_SEG, sw), lambda i: (0, i, 0))] * 2
        args += [h0[0], h0[1]]
    scratch = [
        pltpu.VMEM((n_cb, t_rows, LANES), BF16),
        pltpu.VMEM((n_cb, t_rows, LANES), BF16),
        pltpu.VMEM((t_rows, cw), F32),
        pltpu.VMEM((n_cb, t_rows, LANES), BF16),
    ]
    if chain:
        scratch += [pltpu.VMEM((n_cb, SSM_SEG, sw), F32)] * 2
    state_shape = jax.ShapeDtypeStruct((n_cb, n_chunks * SSM_SEG, sw), F32)
    return pl.pallas_call(
        functools.partial(_s5_kernel, chain=chain, seg_len=seg_len, chunks_per_seq=chunks_per_seq,
                          n_cb=n_cb, cw=cw),
        grid=(n_chunks,),
        in_specs=in_specs,
        out_specs=[
            pl.BlockSpec((t_rows, d_model), lambda i: (i, 0)),
            pl.BlockSpec((n_cb, SSM_SEG, sw), lambda i: (0, i, 0)),
            pl.BlockSpec((n_cb, SSM_SEG, sw), lambda i: (0, i, 0)),
        ],
        out_shape=[jax.ShapeDtypeStruct((m_rows, d_model), BF16), state_shape, state_shape],
        scratch_shapes=scratch,
        compiler_params=_params("arbitrary"),
        name="s5_chain" if chain else "s5_step",
    )(*args)


def _ssm_tables(lam_re, lam_im, log_step, b_re, b_im, c_re, c_im, d, seg_len):
    n_groups, n_state, n_chan = b_re.shape
    gpb = LANES // n_chan
    n_cb = n_groups // gpb
    sw = gpb * n_state
    lam_re = lam_re.astype(F32)
    lam_im = lam_im.astype(F32)
    dt = jnp.exp(log_step.astype(F32))[:, None]
    mag = jnp.exp(lam_re * dt)
    a_re = mag * jnp.cos(lam_im * dt)
    a_im = mag * jnp.sin(lam_im * dt)
    den = lam_re * lam_re + lam_im * lam_im
    q_re = ((a_re - 1.0) * lam_re + a_im * lam_im) / den
    q_im = (a_im * lam_re - (a_re - 1.0) * lam_im) / den
    bb_re = q_re[..., None] * b_re.astype(F32) - q_im[..., None] * b_im.astype(F32)
    bb_im = q_re[..., None] * b_im.astype(F32) + q_im[..., None] * b_re.astype(F32)
    eye = jnp.eye(gpb, dtype=F32)

    def bmat(bb):
        t = bb.reshape(n_cb, gpb, n_state, n_chan)
        t = jnp.einsum('cgnp,gh->cgphn', t, eye)
        return t.reshape(n_cb, gpb * n_chan, sw)

    def cmat(cc):
        t = cc.astype(F32).reshape(n_cb, gpb, n_chan, n_state)
        t = jnp.einsum('cgpn,gh->cgnhp', t, eye)
        return t.reshape(n_cb, sw, gpb * n_chan)

    b_full = jnp.concatenate([bmat(bb_re), bmat(bb_im)], axis=-1)
    b_hi = b_full.astype(BF16)
    b_lo = (b_full - b_hi.astype(F32)).astype(BF16)
    c_full = jnp.concatenate([cmat(c_re), -cmat(c_im)], axis=1).astype(BF16)
    rep = lambda a: jnp.broadcast_to(a.reshape(n_cb, 1, sw), (n_cb, SSM_SEG, sw))
    t_rows = SSM_SEG * seg_len
    r = jnp.arange(t_rows)
    src = (r % SSM_SEG) * seg_len + r // SSM_SEG
    perm = (src[:, None] == jnp.arange(t_rows)[None, :]).astype(BF16)
    return dict(b_hi=b_hi, b_lo=b_lo, c=c_full, a_re=rep(a_re), a_im=rep(a_im),
                d=d.astype(F32).reshape(n_cb, 1, LANES), perm=perm, perm_t=perm.T)


def _glu_kernel(a_ref, wv_ref, wg_ref, x_ref, o_ref):
    a = a_ref[...]
    zv = _dot(a, wv_ref[...])
    zg = _dot(a, wg_ref[...])
    o_ref[...] = x_ref[...] + zv * jax.nn.sigmoid(zg)


def _glu_call(a, w, x, *, tm, tn):
    m_rows, d_in = a.shape
    d_out = x.shape[1]
    nj = d_out // tn
    return pl.pallas_call(
        _glu_kernel,
        grid=(m_rows // tm, nj),
        in_specs=[
            pl.BlockSpec((tm, d_in), lambda i, j: (i, 0)),
            pl.BlockSpec((d_in, tn), lambda i, j: (0, j)),
            pl.BlockSpec((d_in, tn), lambda i, j: (0, j + nj)),
            pl.BlockSpec((tm, tn), lambda i, j: (i, j)),
        ],
        out_specs=pl.BlockSpec((tm, tn), lambda i, j: (i, j)),
        out_shape=jax.ShapeDtypeStruct((m_rows, d_out), F32),
        compiler_params=_params("parallel", "arbitrary"),
        name="glu",
    )(a, w, w, x)


def _ffn_kernel(x_ref, g_ref, wa_ref, wb_ref, wo_ref, o_ref, xn_scr):
    @pl.when(pl.program_id(1) == 0)
    def _():
        x = x_ref[...]
        xn_scr[...] = _rms(x, g_ref[...]).astype(BF16)
        o_ref[...] = x

    xn = xn_scr[...]
    a = _dot(xn, wa_ref[...])
    b = _dot(xn, wb_ref[...])
    h = (a * jax.nn.sigmoid(a) * b).astype(BF16)
    o_ref[...] += _dot(h, wo_ref[...])


def _ffn_call(x, g, w_in, w_out, *, tm, tf):
    m_rows, d_model = x.shape
    d_ff = w_out.shape[0]
    nj = d_ff // tf
    return pl.pallas_call(
        _ffn_kernel,
        grid=(m_rows // tm, nj),
        in_specs=[
            pl.BlockSpec((tm, d_model), lambda i, j: (i, 0)),
            pl.BlockSpec((1, d_model), lambda i, j: (0, 0)),
            pl.BlockSpec((d_model, tf), lambda i, j: (0, j)),
            pl.BlockSpec((d_model, tf), lambda i, j: (0, j + nj)),
            pl.BlockSpec((tf, d_model), lambda i, j: (j, 0)),
        ],
        out_specs=pl.BlockSpec((tm, d_model), lambda i, j: (i, 0)),
        out_shape=jax.ShapeDtypeStruct((m_rows, d_model), F32),
        scratch_shapes=[pltpu.VMEM((tm, d_model), BF16)],
        compiler_params=_params("parallel", "arbitrary"),
        name="ffn",
    )(x, g, w_in, w_in, w_out)


def _proj_kernel(*refs, head_norm, scale, n_out, head_dim):
    if head_norm:
        x_ref, g_ref, w_ref, hn_ref, cos_ref, sin_ref = refs[:6]
        out_refs = refs[6:6 + n_out]
    else:
        x_ref, g_ref, w_ref = refs[:3]
        out_refs = refs[3:3 + n_out]
    xn_scr = refs[-1]

    @pl.when(pl.program_id(1) == 0)
    def _():
        xn_scr[...] = _rms(x_ref[...], g_ref[...]).astype(BF16)

    y = _dot(xn_scr[...], w_ref[...])
    if not head_norm:
        for o_ref in out_refs:
            o_ref[...] = y.astype(o_ref.dtype)
        return

    tm, tn = y.shape
    half = head_dim // 2
    hd_shift = head_dim.bit_length() - 1
    gi = lax.broadcasted_iota(jnp.int32, (LANES, LANES), 0) >> hd_shift
    gj = lax.broadcasted_iota(jnp.int32, (LANES, LANES), 1) >> hd_shift
    gmat = (gi == gj).astype(BF16)
    lane = lax.broadcasted_iota(jnp.int32, (tm, LANES), 1)
    first_half = (lane & (head_dim - 1)) < half
    cos = cos_ref[...]
    sin = sin_ref[...]
    hn = hn_ref[...]
    for s in range(tn // LANES):
        ys = y[:, s * LANES:(s + 1) * LANES]
        sq_hi, sq_lo = _split_bf16(ys * ys)
        ss = _dot(sq_hi, gmat) + _dot(sq_lo, gmat)
        yn = ys * lax.rsqrt(ss * (1.0 / head_dim) + EPS) * hn
        partner = jnp.where(first_half, pltpu.roll(yn, LANES - half, 1), pltpu.roll(yn, half, 1))
        r = yn * cos + partner * sin
        if scale != 1.0:
            r = r * scale
        for o_ref in out_refs:
            o_ref[:, s * LANES:(s + 1) * LANES] = r.astype(o_ref.dtype)


def _proj_call(x, g, w, col0, n_cols, out_dtypes, *, tm, tn, rope=None, scale=1.0, head_dim=64, name="proj"):
    m_rows, d_model = x.shape
    nj = n_cols // tn
    j0 = col0 // tn
    head_norm = rope is not None
    in_specs = [
        pl.BlockSpec((tm, d_model), lambda i, j: (i, 0)),
        pl.BlockSpec((1, d_model), lambda i, j: (0, 0)),
        pl.BlockSpec((d_model, tn), lambda i, j: (0, j + j0)),
    ]
    args = [x, g, w]
    if head_norm:
        hn, cos, sin = rope
        nrep = cos.shape[0] // tm
        in_specs += [
            pl.BlockSpec((1, LANES), lambda i, j: (0, 0)),
            pl.BlockSpec((tm, LANES), lambda i, j: (i % nrep, 0)),
            pl.BlockSpec((tm, LANES), lambda i, j: (i % nrep, 0)),
        ]
        args += [hn, cos, sin]
    return pl.pallas_call(
        functools.partial(_proj_kernel, head_norm=head_norm, scale=scale, n_out=len(out_dtypes),
                          head_dim=head_dim),
        grid=(m_rows // tm, nj),
        in_specs=in_specs,
        out_specs=[pl.BlockSpec((tm, tn), lambda i, j: (i, j)) for _ in out_dtypes],
        out_shape=[jax.ShapeDtypeStruct((m_rows, n_cols), dt) for dt in out_dtypes],
        scratch_shapes=[pltpu.VMEM((tm, d_model), BF16)],
        compiler_params=_params("parallel", "arbitrary"),
        name=name,
    )(*args)


def _resmm_kernel(a_ref, w_ref, x_ref, o_ref):
    o_ref[...] = x_ref[...] + _dot(a_ref[...], w_ref[...])


def _resmm_call(a, w, x, *, tm, tn):
    m_rows, d_in = a.shape
    d_out = w.shape[1]
    return pl.pallas_call(
        _resmm_kernel,
        grid=(m_rows // tm, d_out // tn),
        in_specs=[
            pl.BlockSpec((tm, d_in), lambda i, j: (i, 0)),
            pl.BlockSpec((d_in, tn), lambda i, j: (0, j)),
            pl.BlockSpec((tm, tn), lambda i, j: (i, j)),
        ],
        out_specs=pl.BlockSpec((tm, tn), lambda i, j: (i, j)),
        out_shape=jax.ShapeDtypeStruct((m_rows, d_out), F32),
        compiler_params=_params("parallel", "arbitrary"),
        name="out_proj",
    )(a, w, x)


def _stack_maps(q, head_dim):
    lane = lax.broadcasted_iota(jnp.int32, q.shape, 1)
    zero = jnp.zeros_like(q)
    return jnp.concatenate([jnp.where(lane < head_dim, q, zero), jnp.where(lane >= head_dim, q, zero)], axis=0)


def _softmax_step(s, v, m_prev, l_prev, acc_prev):
    m_new = jnp.maximum(m_prev, jnp.max(s, axis=-1, keepdims=True))
    alpha = jnp.exp(m_prev - m_new)
    p = jnp.exp(s - m_new[:, 0:1])
    l_new = alpha * l_prev + jnp.sum(p, axis=-1, keepdims=True)
    acc_new = alpha * acc_prev + _dot(p.astype(BF16), v)
    return m_new, l_new, acc_new


def _finish_heads(acc, l, lam, sub, out_scale, rows):
    o = acc / l
    o = o[:rows] - lam * o[rows:]
    return _rms(o, sub) * out_scale


def _attn_prompt_kernel(lam_ref, sub_ref, q_ref, k_ref, v_ref, o_ref, m_scr, l_scr, acc_scr,
                        *, tq, head_dim, out_scale):
    qi = pl.program_id(2)
    qq = _stack_maps(q_ref[...], head_dim)
    m_scr[...] = jnp.full_like(m_scr, NEG)
    l_scr[...] = jnp.zeros_like(l_scr)
    acc_scr[...] = jnp.zeros_like(acc_scr)

    def step(kj, masked):
        start = pl.multiple_of(kj * tq, tq)
        k = k_ref[pl.ds(start, tq), :]
        v = v_ref[pl.ds(start, tq), :]
        s = _dot_nt(qq, k)
        if masked:
            row = lax.broadcasted_iota(jnp.int32, s.shape, 0)
            row = jnp.where(row >= tq, row - tq, row)
            col = lax.broadcasted_iota(jnp.int32, s.shape, 1)
            s = jnp.where((col >> CHUNK_SHIFT) <= (row >> CHUNK_SHIFT), s, NEG)
        m_new, l_new, acc_new = _softmax_step(s, v, m_scr[...], l_scr[...], acc_scr[...])
        m_scr[...] = m_new
        l_scr[...] = l_new
        acc_scr[...] = acc_new

    def body(kj, carry):
        step(kj, False)
        return carry

    lax.fori_loop(0, qi, body, 0)
    step(qi, True)
    o = _finish_heads(acc_scr[...], l_scr[...], lam_ref[...], sub_ref[...], out_scale, tq)
    o_ref[...] = o.astype(o_ref.dtype)


def _attn_prompt_call(lam, sub, q, k, v, *, n_batch, seq, n_heads, head_dim, tq, out_scale):
    nq = seq // tq
    hw = 2 * head_dim
    return pl.pallas_call(
        functools.partial(_attn_prompt_kernel, tq=tq, head_dim=head_dim, out_scale=out_scale),
        grid=(n_batch, n_heads, nq),
        in_specs=[
            pl.BlockSpec((1, LANES), lambda b, h, i: (0, 0)),
            pl.BlockSpec((1, hw), lambda b, h, i: (0, 0)),
            pl.BlockSpec((tq, hw), lambda b, h, i: (b * nq + i, h)),
            pl.BlockSpec((seq, hw), lambda b, h, i: (b, h)),
            pl.BlockSpec((seq, hw), lambda b, h, i: (b, h)),
        ],
        out_specs=pl.BlockSpec((tq, hw), lambda b, h, i: (b * nq + i, h)),
        out_shape=jax.ShapeDtypeStruct(q.shape, BF16),
        scratch_shapes=[pltpu.VMEM((2 * tq, LANES), F32), pltpu.VMEM((2 * tq, LANES), F32),
                        pltpu.VMEM((2 * tq, hw), F32)],
        compiler_params=_params("parallel", "parallel", "arbitrary"),
        name="attn_prompt",
    )(lam, sub, q, k, v)


def _attn_sample_kernel(lam_ref, sub_ref, q_ref, ck_ref, cv_ref, nk_ref, nv_ref, o_ref, m_scr, l_scr, acc_scr,
                        *, n_heads, head_dim, n_cache_tiles, past_len, out_scale):
    j = pl.program_id(1)
    hw = 2 * head_dim
    rows = q_ref.shape[0]

    @pl.when(j == 0)
    def _():
        m_scr[...] = jnp.full_like(m_scr, NEG)
        l_scr[...] = jnp.zeros_like(l_scr)
        acc_scr[...] = jnp.zeros_like(acc_scr)

    def heads(k_ref, v_ref, mask):
        for h in range(n_heads):
            cs = slice(h * hw, (h + 1) * hw)
            qq = _stack_maps(q_ref[:, cs], head_dim)
            s = _dot_nt(qq, k_ref[:, cs].astype(BF16))
            if mask is not None:
                s = jnp.where(mask, s, NEG)
            m_new, l_new, acc_new = _softmax_step(s, v_ref[:, cs].astype(BF16), m_scr[h], l_scr[h], acc_scr[h])
            m_scr[h] = m_new
            l_scr[h] = l_new
            acc_scr[h] = acc_new

    @pl.when(j < n_cache_tiles)
    def _():
        heads(ck_ref, cv_ref, None)

    @pl.when(j == n_cache_tiles)
    def _():
        n_new = nk_ref.shape[0]
        row = lax.broadcasted_iota(jnp.int32, (2 * rows, n_new), 0)
        row = jnp.where(row >= rows, row - rows, row)
        col = lax.broadcasted_iota(jnp.int32, (2 * rows, n_new), 1)
        heads(nk_ref, nv_ref, ((past_len + col) >> CHUNK_SHIFT) <= ((past_len + row) >> CHUNK_SHIFT))
        lam = lam_ref[...]
        sub = sub_ref[...]
        for h in range(n_heads):
            o = _finish_heads(acc_scr[h], l_scr[h], lam, sub, out_scale, rows)
            o_ref[:, h * hw:(h + 1) * hw] = o.astype(o_ref.dtype)


def _attn_sample_call(lam, sub, q, cache_k, cache_v, new_k, new_v, *, n_batch, dec_seq, past_len, n_heads,
                      head_dim, tk, out_scale):
    d_model = q.shape[1]
    hw = 2 * head_dim
    n_tiles = past_len // tk
    last = n_tiles - 1
    cache_map = lambda b, j: (b * n_tiles + jnp.minimum(j, last), 0)
    return pl.pallas_call(
        functools.partial(_attn_sample_kernel, n_heads=n_heads, head_dim=head_dim, n_cache_tiles=n_tiles,
                          past_len=past_len, out_scale=out_scale),
        grid=(n_batch, n_tiles + 1),
        in_specs=[
            pl.BlockSpec((1, LANES), lambda b, j: (0, 0)),
            pl.BlockSpec((1, hw), lambda b, j: (0, 0)),
            pl.BlockSpec((dec_seq, d_model), lambda b, j: (b, 0)),
            pl.BlockSpec((tk, d_model), cache_map),
            pl.BlockSpec((tk, d_model), cache_map),
            pl.BlockSpec((dec_seq, d_model), lambda b, j: (b, 0)),
            pl.BlockSpec((dec_seq, d_model), lambda b, j: (b, 0)),
        ],
        out_specs=pl.BlockSpec((dec_seq, d_model), lambda b, j: (b, 0)),
        out_shape=jax.ShapeDtypeStruct(q.shape, BF16),
        scratch_shapes=[pltpu.VMEM((n_heads, 2 * dec_seq, LANES), F32),
                        pltpu.VMEM((n_heads, 2 * dec_seq, LANES), F32),
                        pltpu.VMEM((n_heads, 2 * dec_seq, hw), F32)],
        compiler_params=_params("parallel", "arbitrary"),
        name="attn_sample",
    )(lam, sub, q, cache_k, cache_v, new_k, new_v)


def _rope_tables(pos, head_dim, reps):
    half = head_dim // 2
    inv = ROPE_THETA ** (-jnp.arange(half, dtype=F32) / half)
    ang = pos.astype(F32)[:, None] * inv[None, :]
    cos = jnp.cos(ang)
    sin = jnp.sin(ang)
    n_groups = LANES // head_dim
    cos_t = jnp.tile(cos, (reps, 2 * n_groups))
    sin_t = jnp.tile(jnp.concatenate([-sin, sin], axis=-1), (reps, n_groups))
    return cos_t, sin_t


def _row_tile(m_rows, cap):
    t = cap
    while m_rows % t:
        t //= 2
    return t


def _trunk(x, pos, n_seq, h0, cache, p, prompt):
    m_rows, d_model = x.shape
    slen = m_rows // n_seq
    n_a = p['norm_ssm'].shape[0]
    depth = n_a + p['norm_attn'].shape[0]
    head_dim = p['k_norm'].shape[0]
    hw = 2 * head_dim
    n_heads = d_model // hw
    k_cols = n_heads * hw
    tm = _row_tile(m_rows, 512)
    tm_big = _row_tile(m_rows, 1024)
    d_ff = p['w_ffn_out'][0].shape[0]
    tf = 512 if d_ff % 512 == 0 else 256
    row = lambda a: a.astype(F32).reshape(1, -1)

    if prompt:
        seg_len = 32
        rope_reps = 1
    else:
        seg_len = slen
        rope_reps = tm // slen
    cos_t, sin_t = _rope_tables(pos, head_dim, rope_reps)

    new_h = []
    k_f32 = v_f32 = k_b = v_b = None
    for layer in range(depth):
        if layer < n_a:
            i = layer
            tabs = _ssm_tables(p['ssm_lam_re'][i], p['ssm_lam_im'][i], p['ssm_log_step'][i], p['ssm_b_re'][i],
                               p['ssm_b_im'][i], p['ssm_c_re'][i], p['ssm_c_im'][i], p['ssm_d'][i], seg_len)
            n_cb, _, sw = tabs['a_re'].shape
            if prompt:
                t_rows = SSM_SEG * seg_len
                gy, h_re, h_im = _s5_call(x, row(p['norm_ssm'][i]), tabs, None, seg_len=seg_len,
                                          chunks_per_seq=slen // t_rows)
                last = (jnp.arange(n_seq) + 1) * (slen // seg_len) - 1
                h_re, h_im = h_re[:, last], h_im[:, last]
            else:
                to_cb = lambda a: a.astype(F32).reshape(n_seq, n_cb, sw).transpose(1, 0, 2)
                gy, h_re, h_im = _s5_call(x, row(p['norm_ssm'][i]), tabs, (to_cb(h0[0][i]), to_cb(h0[1][i])),
                                          seg_len=seg_len, chunks_per_seq=1)
            from_cb = lambda a: a.transpose(1, 0, 2).reshape(n_seq, -1)
            new_h.append((from_cb(h_re), from_cb(h_im)))
            x = _glu_call(gy, p['w_glu'][i], x, tm=tm_big, tn=512)
        else:
            if layer == n_a:
                g_kv = row(p['norm_kv'])
                hn_k = jnp.tile(row(p['k_norm']), (1, LANES // head_dim))
                k_f32, k_b = _proj_call(x, g_kv, p['w_kv'], 0, k_cols, (F32, BF16), tm=tm, tn=512,
                                        rope=(hn_k, cos_t, sin_t), head_dim=head_dim, name="k_proj")
                v_f32, v_b = _proj_call(x, g_kv, p['w_kv'], k_cols, p['w_kv'].shape[1] - k_cols, (F32, BF16),
                                        tm=tm, tn=512, name="v_proj")
            j = layer - n_a
            lambda_init = 0.8 - 0.6 * math.exp(-0.3 * layer)
            hn_q = jnp.tile(row(p['q_norm'][j]), (1, LANES // head_dim))
            (q,) = _proj_call(x, row(p['norm_attn'][j]), p['w_q'][j], 0, k_cols, (BF16,), tm=tm, tn=512,
                              rope=(hn_q, cos_t, sin_t), scale=head_dim ** -0.5, head_dim=head_dim,
                              name="q_proj")
            lam = (jnp.exp(jnp.sum(p['lambda_q1'][j].astype(F32) * p['lambda_k1'][j].astype(F32)))
                   - jnp.exp(jnp.sum(p['lambda_q2'][j].astype(F32) * p['lambda_k2'][j].astype(F32)))
                   + lambda_init)
            lam = jnp.full((1, LANES), lam, F32)
            sub = row(p['subln'][j])
            if prompt:
                o = _attn_prompt_call(lam, sub, q, k_b, v_b, n_batch=n_seq, seq=slen, n_heads=n_heads,
                                      head_dim=head_dim, tq=256, out_scale=1.0 - lambda_init)
            else:
                cache_k, cache_v = cache
                past_len = cache_k.shape[1]
                o = _attn_sample_call(lam, sub, q, cache_k.reshape(n_seq * past_len, d_model),
                                      cache_v.reshape(n_seq * past_len, d_model), k_b, v_b, n_batch=n_seq,
                                      dec_seq=slen, past_len=past_len, n_heads=n_heads, head_dim=head_dim,
                                      tk=512, out_scale=1.0 - lambda_init)
            x = _resmm_call(o, p['w_o'][j], x, tm=tm_big, tn=512)
        x = _ffn_call(x, row(p['norm_ffn'][layer]), p['w_ffn_in'][layer], p['w_ffn_out'][layer], tm=tm, tf=tf)
    return x, new_h, k_f32, v_f32


def kernel(x_prompt, x_sample, state_ssm_re, state_ssm_im, cache_k, cache_v, norm_ssm, ssm_lam_re, ssm_lam_im, ssm_log_step, ssm_b_re, ssm_b_im, ssm_c_re, ssm_c_im, ssm_d, w_glu, norm_kv, w_kv, k_norm, norm_attn, w_q, q_norm, lambda_q1, lambda_k1, lambda_q2, lambda_k2, subln, w_o, norm_ffn, w_ffn_in, w_ffn_out):
    n_b, seq, d_model = x_prompt.shape
    n_db, dec_seq, _ = x_sample.shape
    past_len = cache_k.shape[1]
    head_dim = k_norm.shape[0]
    n_heads = d_model // (2 * head_dim)
    n_groups, n_state = ssm_lam_re.shape[1:]
    assert ssm_b_re.shape[3] * n_groups == d_model and LANES % ssm_b_re.shape[3] == 0
    assert 2 * head_dim == LANES and seq % (SSM_SEG * 32) == 0 and n_db % SSM_SEG == 0
    assert past_len % CHUNK == 0 and dec_seq % 16 == 0

    p = dict(norm_ssm=norm_ssm, ssm_lam_re=ssm_lam_re, ssm_lam_im=ssm_lam_im, ssm_log_step=ssm_log_step,
             ssm_b_re=ssm_b_re, ssm_b_im=ssm_b_im, ssm_c_re=ssm_c_re, ssm_c_im=ssm_c_im, ssm_d=ssm_d,
             w_glu=w_glu.astype(BF16), norm_kv=norm_kv, w_kv=w_kv.astype(BF16), k_norm=k_norm,
             norm_attn=norm_attn, w_q=w_q.astype(BF16), q_norm=q_norm, lambda_q1=lambda_q1,
             lambda_k1=lambda_k1, lambda_q2=lambda_q2, lambda_k2=lambda_k2, subln=subln,
             w_o=w_o.astype(BF16), norm_ffn=norm_ffn, w_ffn_in=w_ffn_in.astype(BF16),
             w_ffn_out=w_ffn_out.astype(BF16))

    pos_p = jnp.arange(seq, dtype=jnp.int32)
    y_p, h_p, k_p, v_p = _trunk(x_prompt.reshape(n_b * seq, d_model), pos_p, n_b, None, None, p, True)
    pos_s = past_len + jnp.arange(dec_seq, dtype=jnp.int32)
    y_s, h_s, k_s, v_s = _trunk(x_sample.reshape(n_db * dec_seq, d_model), pos_s, n_db,
                                (state_ssm_re, state_ssm_im), (cache_k, cache_v), p, False)

    def states(hs, n_seq, part):
        return jnp.stack([h[part].reshape(n_seq, n_groups, n_state) for h in hs])

    return (y_p.reshape(n_b, seq, d_model), y_s.reshape(n_db, dec_seq, d_model),
            states(h_p, n_b, 0), states(h_p, n_b, 1),
            k_p.reshape(n_b, seq, n_heads, 2, head_dim), v_p.reshape(n_b, seq, n_heads, 2 * head_dim),
            states(h_s, n_db, 0), states(h_s, n_db, 1),
            k_s.reshape(n_db, dec_seq, n_heads, 2, head_dim), v_s.reshape(n_db, dec_seq, n_heads, 2 * head_dim))
```

```python
import functools
import math

import jax
import jax.numpy as jnp
from jax import lax
from jax.experimental import pallas as pl
from jax.experimental.pallas import tpu as pltpu

CHUNK = 64
CHUNK_SHIFT = CHUNK.bit_length() - 1
EPS = 1e-6
ROPE_THETA = 10000.0
LANES = 128
SUBLANES = 8
SSM_SEG = SUBLANES
VMEM_LIMIT = 56 * 1024 * 1024

F32 = jnp.float32
BF16 = jnp.bfloat16
NEG = -1e30
LOG2_E = math.log2(math.e)


def _dot(a, b):
    return jnp.dot(a, b, preferred_element_type=F32)


def _dot_nt(a, b):
    return lax.dot_general(a, b, (((1,), (1,)), ((), ())), preferred_element_type=F32)


def _rms(x, g):
    return x * lax.rsqrt(jnp.mean(x * x, axis=-1, keepdims=True) + EPS) * g


def _split_bf16(x):
    hi = x.astype(BF16)
    lo = (x - hi.astype(F32)).astype(BF16)
    return hi, lo


def _params(*sem):
    return pltpu.CompilerParams(dimension_semantics=sem, vmem_limit_bytes=VMEM_LIMIT)


def _cmul(ar, ai, br, bi):
    return ar * br - ai * bi, ar * bi + ai * br


def _cpow(ar, ai, n):
    rr, ri = None, None
    br, bi = ar, ai
    while n:
        if n & 1:
            rr, ri = (br, bi) if rr is None else _cmul(rr, ri, br, bi)
        n >>= 1
        if n:
            br, bi = _cmul(br, bi, br, bi)
    return rr, ri


def _s5_kernel(*refs, chain, seg_len, chunks_per_seq, n_cb, cw):
    if chain:
        (x_ref, g_ref, p_ref, pt_ref, bhi_ref, blo_ref, c_ref, are_ref, aim_ref, d_ref,
         gy_ref, hre_ref, him_ref, uhi_scr, ulo_scr, bu_scr, gyp_scr, cre_scr, cim_scr) = refs
    else:
        (x_ref, g_ref, p_ref, pt_ref, bhi_ref, blo_ref, c_ref, are_ref, aim_ref, d_ref,
         h0re_ref, h0im_ref, gy_ref, hre_ref, him_ref, uhi_scr, ulo_scr, bu_scr, gyp_scr) = refs
    sw = cw // 2

    u = _rms(x_ref[...], g_ref[...])
    u_hi, u_lo = _split_bf16(u)
    perm = p_ref[...]
    up_hi = _dot(perm, u_hi).astype(BF16)
    up_lo = _dot(perm, u_lo).astype(BF16)
    for cb in range(n_cb):
        uhi_scr[cb] = up_hi[:, cb * LANES:(cb + 1) * LANES]
        ulo_scr[cb] = up_lo[:, cb * LANES:(cb + 1) * LANES]

    if chain:
        @pl.when(pl.program_id(0) % chunks_per_seq == 0)
        def _():
            cre_scr[...] = jnp.zeros_like(cre_scr)
            cim_scr[...] = jnp.zeros_like(cim_scr)

    row_id = lax.broadcasted_iota(jnp.int32, (SSM_SEG, sw), 0)

    def colblock(cb, carry):
        uh = uhi_scr[cb]
        ul = ulo_scr[cb]
        bh = bhi_ref[cb]
        bu_scr[...] = _dot(uh, bh) + _dot(ul, bh) + _dot(uh, blo_ref[cb])
        are = are_ref[cb]
        aim = aim_ref[cb]
        if chain:
            hre = jnp.zeros((SSM_SEG, sw), F32)
            him = jnp.zeros((SSM_SEG, sw), F32)
        else:
            hre = h0re_ref[cb]
            him = h0im_ref[cb]
        for k in range(seg_len):
            r0 = k * SSM_SEG
            nre = are * hre - aim * him + bu_scr[r0:r0 + SSM_SEG, 0:sw]
            nim = are * him + aim * hre + bu_scr[r0:r0 + SSM_SEG, sw:cw]
            hre, him = nre, nim
            bu_scr[r0:r0 + SSM_SEG, 0:sw] = hre
            bu_scr[r0:r0 + SSM_SEG, sw:cw] = him

        if chain:
            pre, pim = _cpow(are[0:1], aim[0:1], seg_len)
            cur_re = cre_scr[cb][SSM_SEG - 1:SSM_SEG]
            cur_im = cim_scr[cb][SSM_SEG - 1:SSM_SEG]
            prev_re = jnp.zeros((SSM_SEG, sw), F32)
            prev_im = jnp.zeros((SSM_SEG, sw), F32)
            end_re = jnp.zeros((SSM_SEG, sw), F32)
            end_im = jnp.zeros((SSM_SEG, sw), F32)
            for j in range(SSM_SEG):
                prev_re = jnp.where(row_id == j, cur_re, prev_re)
                prev_im = jnp.where(row_id == j, cur_im, prev_im)
                mre, mim = _cmul(pre, pim, cur_re, cur_im)
                cur_re = mre + hre[j:j + 1]
                cur_im = mim + him[j:j + 1]
                end_re = jnp.where(row_id == j, cur_re, end_re)
                end_im = jnp.where(row_id == j, cur_im, end_im)
            cre_scr[cb] = end_re
            cim_scr[cb] = end_im
            hre_ref[cb] = end_re
            him_ref[cb] = end_im
            qre, qim = _cmul(are, aim, prev_re, prev_im)
            for k in range(seg_len):
                r0 = k * SSM_SEG
                bu_scr[r0:r0 + SSM_SEG, 0:sw] = bu_scr[r0:r0 + SSM_SEG, 0:sw] + qre
                bu_scr[r0:r0 + SSM_SEG, sw:cw] = bu_scr[r0:r0 + SSM_SEG, sw:cw] + qim
                if k + 1 < seg_len:
                    qre, qim = _cmul(are, aim, qre, qim)
        else:
            hre_ref[cb] = hre
            him_ref[cb] = him

        y = _dot(bu_scr[...].astype(BF16), c_ref[cb])
        y = y + d_ref[cb] * (uh.astype(F32) + ul.astype(F32))
        gyp_scr[cb] = jax.nn.gelu(y, approximate=True).astype(BF16)
        return carry

    lax.fori_loop(0, n_cb, colblock, 0)

    perm_t = pt_ref[...]
    for cb in range(n_cb):
        gy_ref[:, cb * LANES:(cb + 1) * LANES] = _dot(perm_t, gyp_scr[cb]).astype(BF16)


def _s5_call(x, g, ssm, h0, *, seg_len, chunks_per_seq):
    m_rows, d_model = x.shape
    chain = h0 is None
    t_rows = SSM_SEG * seg_len
    n_chunks = m_rows // t_rows
    n_cb, _, cw = ssm['b_hi'].shape
    sw = cw // 2
    full3 = lambda i: (0, 0, 0)
    full2 = lambda i: (0, 0)
    in_specs = [
        pl.BlockSpec((t_rows, d_model), lambda i: (i, 0)),
        pl.BlockSpec((1, d_model), full2),
        pl.BlockSpec((t_rows, t_rows), full2),
        pl.BlockSpec((t_rows, t_rows), full2),
        pl.BlockSpec((n_cb, LANES, cw), full3),
        pl.BlockSpec((n_cb, LANES, cw), full3),
        pl.BlockSpec((n_cb, cw, LANES), full3),
        pl.BlockSpec((n_cb, SSM_SEG, sw), full3),
        pl.BlockSpec((n_cb, SSM_SEG, sw), full3),
        pl.BlockSpec((n_cb, 1, LANES), full3),
    ]
    args = [x, g, ssm['perm'], ssm['perm_t'], ssm['b_hi'], ssm['b_lo'], ssm['c'], ssm['a_re'], ssm['a_im'],
            ssm['d']]
    if not chain:
        in_specs += [pl.BlockSpec((n_cb, SSM_SEG, sw), lambda i: (0, i, 0))] * 2
        args += [h0[0], h0[1]]
    scratch = [
        pltpu.VMEM((n_cb, t_rows, LANES), BF16),
        pltpu.VMEM((n_cb, t_rows, LANES), BF16),
        pltpu.VMEM((t_rows, cw), F32),
        pltpu.VMEM((n_cb, t_rows, LANES), BF16),
    ]
    if chain:
        scratch += [pltpu.VMEM((n_cb, SSM_SEG, sw), F32)] * 2
    state_shape = jax.ShapeDtypeStruct((n_cb, n_chunks * SSM_SEG, sw), F32)
    return pl.pallas_call(
        functools.partial(_s5_kernel, chain=chain, seg_len=seg_len, chunks_per_seq=chunks_per_seq,
                          n_cb=n_cb, cw=cw),
        grid=(n_chunks,),
        in_specs=in_specs,
        out_specs=[
            pl.BlockSpec((t_rows, d_model), lambda i: (i, 0)),
            pl.BlockSpec((n_cb, SSM_SEG, sw), lambda i: (0, i, 0)),
            pl.BlockSpec((n_cb, SSM_SEG, sw), lambda i: (0, i, 0)),
        ],
        out_shape=[jax.ShapeDtypeStruct((m_rows, d_model), BF16), state_shape, state_shape],
        scratch_shapes=scratch,
        compiler_params=_params("arbitrary"),
        name="s5_chain" if chain else "s5_step",
    )(*args)


def _ssm_tables(lam_re, lam_im, log_step, b_re, b_im, c_re, c_im, d, seg_len):
    n_groups, n_state, n_chan = b_re.shape
    gpb = LANES // n_chan
    n_cb = n_groups // gpb
    sw = gpb * n_state
    lam_re = lam_re.astype(F32)
    lam_im = lam_im.astype(F32)
    dt = jnp.exp(log_step.astype(F32))[:, None]
    mag = jnp.exp(lam_re * dt)
    a_re = mag * jnp.cos(lam_im * dt)
    a_im = mag * jnp.sin(lam_im * dt)
    den = lam_re * lam_re + lam_im * lam_im
    q_re = ((a_re - 1.0) * lam_re + a_im * lam_im) / den
    q_im = (a_im * lam_re - (a_re - 1.0) * lam_im) / den
    bb_re = q_re[..., None] * b_re.astype(F32) - q_im[..., None] * b_im.astype(F32)
    bb_im = q_re[..., None] * b_im.astype(F32) + q_im[..., None] * b_re.astype(F32)
    eye = jnp.eye(gpb, dtype=F32)

    def bmat(bb):
        t = bb.reshape(n_cb, gpb, n_state, n_chan)
        t = jnp.einsum('cgnp,gh->cgphn', t, eye)
        return t.reshape(n_cb, gpb * n_chan, sw)

    def cmat(cc):
        t = cc.astype(F32).reshape(n_cb, gpb, n_chan, n_state)
        t = jnp.einsum('cgpn,gh->cgnhp', t, eye)
        return t.reshape(n_cb, sw, gpb * n_chan)

    b_full = jnp.concatenate([bmat(bb_re), bmat(bb_im)], axis=-1)
    b_hi = b_full.astype(BF16)
    b_lo = (b_full - b_hi.astype(F32)).astype(BF16)
    c_full = jnp.concatenate([cmat(c_re), -cmat(c_im)], axis=1).astype(BF16)
    rep = lambda a: jnp.broadcast_to(a.reshape(n_cb, 1, sw), (n_cb, SSM_SEG, sw))
    t_rows = SSM_SEG * seg_len
    r = jnp.arange(t_rows)
    src = (r % SSM_SEG) * seg_len + r // SSM_SEG
    perm = (src[:, None] == jnp.arange(t_rows)[None, :]).astype(BF16)
    return dict(b_hi=b_hi, b_lo=b_lo, c=c_full, a_re=rep(a_re), a_im=rep(a_im),
                d=d.astype(F32).reshape(n_cb, 1, LANES), perm=perm, perm_t=perm.T)


def _glu_kernel(a_ref, wv_ref, wg_ref, x_ref, o_ref):
    a = a_ref[...]
    zv = _dot(a, wv_ref[...])
    zg = _dot(a, wg_ref[...])
    o_ref[...] = x_ref[...] + zv * jax.nn.sigmoid(zg)


def _glu_call(a, w, x, *, tm, tn):
    m_rows, d_in = a.shape
    d_out = x.shape[1]
    nj = d_out // tn
    return pl.pallas_call(
        _glu_kernel,
        grid=(m_rows // tm, nj),
        in_specs=[
            pl.BlockSpec((tm, d_in), lambda i, j: (i, 0)),
            pl.BlockSpec((d_in, tn), lambda i, j: (0, j)),
            pl.BlockSpec((d_in, tn), lambda i, j: (0, j + nj)),
            pl.BlockSpec((tm, tn), lambda i, j: (i, j)),
        ],
        out_specs=pl.BlockSpec((tm, tn), lambda i, j: (i, j)),
        out_shape=jax.ShapeDtypeStruct((m_rows, d_out), F32),
        compiler_params=_params("parallel", "arbitrary"),
        name="glu",
    )(a, w, w, x)


def _ffn_kernel(x_ref, g_ref, wa_ref, wb_ref, wo_ref, o_ref, xn_scr):
    @pl.when(pl.program_id(1) == 0)
    def _():
        x = x_ref[...]
        xn_scr[...] = _rms(x, g_ref[...]).astype(BF16)
        o_ref[...] = x

    xn = xn_scr[...]
    a = _dot(xn, wa_ref[...])
    b = _dot(xn, wb_ref[...])
    h = (a * jax.nn.sigmoid(a) * b).astype(BF16)
    o_ref[...] += _dot(h, wo_ref[...])


def _ffn_call(x, g, w_in, w_out, *, tm, tf):
    m_rows, d_model = x.shape
    d_ff = w_out.shape[0]
    nj = d_ff // tf
    return pl.pallas_call(
        _ffn_kernel,
        grid=(m_rows // tm, nj),
        in_specs=[
            pl.BlockSpec((tm, d_model), lambda i, j: (i, 0)),
            pl.BlockSpec((1, d_model), lambda i, j: (0, 0)),
            pl.BlockSpec((d_model, tf), lambda i, j: (0, j)),
            pl.BlockSpec((d_model, tf), lambda i, j: (0, j + nj)),
            pl.BlockSpec((tf, d_model), lambda i, j: (j, 0)),
        ],
        out_specs=pl.BlockSpec((tm, d_model), lambda i, j: (i, 0)),
        out_shape=jax.ShapeDtypeStruct((m_rows, d_model), F32),
        scratch_shapes=[pltpu.VMEM((tm, d_model), BF16)],
        compiler_params=_params("parallel", "arbitrary"),
        name="ffn",
    )(x, g, w_in, w_in, w_out)


def _proj_kernel(*refs, head_norm, scale, outs, head_dim):
    n_out = len(outs)
    if head_norm:
        x_ref, g_ref, w_ref, hn_ref, cos_ref, sin_ref = refs[:6]
        out_refs = refs[6:6 + n_out]
    else:
        x_ref, g_ref, w_ref = refs[:3]
        out_refs = refs[3:3 + n_out]
    xn_scr = refs[-1]

    @pl.when(pl.program_id(1) == 0)
    def _():
        xn_scr[...] = _rms(x_ref[...], g_ref[...]).astype(BF16)

    y = _dot(xn_scr[...], w_ref[...])
    tm, tn = y.shape
    if head_norm:
        half = head_dim // 2
        hd_shift = head_dim.bit_length() - 1
        gi = lax.broadcasted_iota(jnp.int32, (LANES, LANES), 0) >> hd_shift
        gj = lax.broadcasted_iota(jnp.int32, (LANES, LANES), 1) >> hd_shift
        gmat = (gi == gj).astype(BF16)
        lane = lax.broadcasted_iota(jnp.int32, (tm, LANES), 1)
        first_half = (lane & (head_dim - 1)) < half
        cos = cos_ref[...]
        sin = sin_ref[...]
        hn = hn_ref[...]
    need_t = any(kind.startswith("cols") for kind in outs)
    for s in range(tn // LANES):
        cs = slice(s * LANES, (s + 1) * LANES)
        r = y[:, cs]
        if head_norm:
            sq_hi, sq_lo = _split_bf16(r * r)
            ss = _dot(sq_hi, gmat) + _dot(sq_lo, gmat)
            yn = r * lax.rsqrt(ss * (1.0 / head_dim) + EPS) * hn
            partner = jnp.where(first_half, pltpu.roll(yn, LANES - half, 1), pltpu.roll(yn, half, 1))
            r = yn * cos + partner * sin
            if scale != 1.0:
                r = r * scale
        rt = r.T if need_t else None
        for kind, o_ref in zip(outs, out_refs):
            if kind == "cols_f32":
                o_ref[cs, :] = rt
            elif kind == "cols_bf16":
                o_ref[0, 0, cs, :] = rt.astype(BF16)
            else:
                o_ref[:, cs] = r.astype(o_ref.dtype)


def _proj_call(x, g, w, col0, n_cols, outs, *, tm, tn, rope=None, scale=1.0, head_dim=64, name="proj", seq=None):
    m_rows, d_model = x.shape
    nj = n_cols // tn
    j0 = col0 // tn
    head_norm = rope is not None
    out_specs, out_shape = [], []
    for kind in outs:
        if kind == "cols_f32":
            tiles = seq // tm
            out_specs.append(pl.BlockSpec((tn, tm), lambda i, j: ((i // tiles) * nj + j, i % tiles)))
            out_shape.append(jax.ShapeDtypeStruct((m_rows // seq * n_cols, seq), F32))
        elif kind == "cols_bf16":
            tiles = seq // tm
            out_specs.append(pl.BlockSpec((1, 1, tn, tm), lambda i, j: (i // tiles, i % tiles, j, 0)))
            out_shape.append(jax.ShapeDtypeStruct((m_rows // seq, tiles, n_cols, tm), BF16))
        else:
            out_specs.append(pl.BlockSpec((tm, tn), lambda i, j: (i, j)))
            out_shape.append(jax.ShapeDtypeStruct((m_rows, n_cols), F32 if kind == "rows_f32" else BF16))
    in_specs = [
        pl.BlockSpec((tm, d_model), lambda i, j: (i, 0)),
        pl.BlockSpec((1, d_model), lambda i, j: (0, 0)),
        pl.BlockSpec((d_model, tn), lambda i, j: (0, j + j0)),
    ]
    args = [x, g, w]
    if head_norm:
        hn, cos, sin = rope
        nrep = cos.shape[0] // tm
        in_specs += [
            pl.BlockSpec((1, LANES), lambda i, j: (0, 0)),
            pl.BlockSpec((tm, LANES), lambda i, j: (i % nrep, 0)),
            pl.BlockSpec((tm, LANES), lambda i, j: (i % nrep, 0)),
        ]
        args += [hn, cos, sin]
    return pl.pallas_call(
        functools.partial(_proj_kernel, head_norm=head_norm, scale=scale, outs=tuple(outs), head_dim=head_dim),
        grid=(m_rows // tm, nj),
        in_specs=in_specs,
        out_specs=out_specs,
        out_shape=out_shape,
        scratch_shapes=[pltpu.VMEM((tm, d_model), BF16)],
        compiler_params=_params("parallel", "arbitrary"),
        name=name,
    )(*args)


def _resmm_kernel(a_ref, w_ref, x_ref, o_ref):
    o_ref[...] = x_ref[...] + _dot(a_ref[...], w_ref[...])


def _resmm_call(a, w, x, *, tm, tn):
    m_rows, d_in = a.shape
    d_out = w.shape[1]
    return pl.pallas_call(
        _resmm_kernel,
        grid=(m_rows // tm, d_out // tn),
        in_specs=[
            pl.BlockSpec((tm, d_in), lambda i, j: (i, 0)),
            pl.BlockSpec((d_in, tn), lambda i, j: (0, j)),
            pl.BlockSpec((tm, tn), lambda i, j: (i, j)),
        ],
        out_specs=pl.BlockSpec((tm, tn), lambda i, j: (i, j)),
        out_shape=jax.ShapeDtypeStruct((m_rows, d_out), F32),
        compiler_params=_params("parallel", "arbitrary"),
        name="out_proj",
    )(a, w, x)


def _stack_maps(q, head_dim):
    lane = lax.broadcasted_iota(jnp.int32, q.shape, 1)
    zero = jnp.zeros_like(q)
    return jnp.concatenate([jnp.where(lane < head_dim, q, zero), jnp.where(lane >= head_dim, q, zero)], axis=0)


ONES_ROWS = 16


def _attn_prompt_kernel(lam_ref, sub_ref, qt_ref, k_ref, vt_ref, o_ref, acc_scr, *, tq, head_dim, out_scale):
    qi = pl.program_id(2)
    hw = 2 * head_dim
    qt = qt_ref[0, 0]
    feat = lax.broadcasted_iota(jnp.int32, qt.shape, 0)
    zero = jnp.zeros_like(qt)
    qqt = jnp.concatenate([jnp.where(feat < head_dim, qt, zero), jnp.where(feat >= head_dim, qt, zero)], axis=1)
    acc_scr[...] = jnp.zeros_like(acc_scr)
    ones = jnp.ones((ONES_ROWS, tq), BF16)

    def step(kj, m_prev, masked):
        start = pl.multiple_of(kj * tq, tq)
        st = _dot(k_ref[pl.ds(start, tq), :], qqt)
        if masked:
            kpos = lax.broadcasted_iota(jnp.int32, st.shape, 0)
            qpos = lax.broadcasted_iota(jnp.int32, st.shape, 1)
            qpos = jnp.where(qpos >= tq, qpos - tq, qpos)
            st = jnp.where((kpos >> CHUNK_SHIFT) <= (qpos >> CHUNK_SHIFT), st, NEG)
        m_new = jnp.maximum(m_prev, jnp.max(st, axis=0, keepdims=True))
        alpha = jnp.exp2(m_prev - m_new)
        pt = jnp.exp2(st - m_new).astype(BF16)
        vt1 = jnp.concatenate([vt_ref[0, kj], ones], axis=0)
        acc_scr[...] = alpha * acc_scr[...] + _dot(vt1, pt)
        return m_new

    m = jnp.full((1, 2 * tq), NEG, F32)
    m = lax.fori_loop(0, qi // 2, lambda j, m: step(2 * j + 1, step(2 * j, m, False), False), m)
    m = lax.cond(qi % 2 == 1, lambda m: step(qi - 1, m, False), lambda m: m, m)
    step(qi, m, True)
    acc = acc_scr[...]
    ot = acc[:hw] * (1.0 / acc[hw:hw + 1])
    ot = ot[:, :tq] - lam_ref[...] * ot[:, tq:]
    ot = ot * lax.rsqrt(jnp.mean(ot * ot, axis=0, keepdims=True) + EPS) * (sub_ref[...] * out_scale)
    o_ref[...] = ot.T.astype(o_ref.dtype)


def _attn_prompt_call(lam, sub, qt, k, vt, *, n_batch, seq, n_heads, head_dim, out_scale):
    tq = qt.shape[3]
    nq = seq // tq
    hw = 2 * head_dim
    return pl.pallas_call(
        functools.partial(_attn_prompt_kernel, tq=tq, head_dim=head_dim, out_scale=out_scale),
        grid=(n_batch, n_heads, nq),
        in_specs=[
            pl.BlockSpec((1, tq), lambda b, h, i: (0, 0)),
            pl.BlockSpec((hw, 1), lambda b, h, i: (0, 0)),
            pl.BlockSpec((1, 1, hw, tq), lambda b, h, i: (b, i, h, 0)),
            pl.BlockSpec((seq, hw), lambda b, h, i: (b, h)),
            pl.BlockSpec((1, nq, hw, tq), lambda b, h, i: (b, 0, h, 0)),
        ],
        out_specs=pl.BlockSpec((tq, hw), lambda b, h, i: (b * nq + i, h)),
        out_shape=jax.ShapeDtypeStruct(k.shape, BF16),
        scratch_shapes=[pltpu.VMEM((hw + ONES_ROWS, 2 * tq), F32)],
        compiler_params=_params("parallel", "parallel", "arbitrary"),
        name="attn_prompt",
    )(lam, sub, qt, k, vt)


def _attn_sample_kernel(lam_ref, sub_ref, q_ref, ck_ref, cv_ref, nk_ref, nv_ref, o_ref, qq_scr, s_scr, m_scr,
                        acc_scr, *, n_heads, head_dim, n_cache_tiles, past_len, out_scale):
    j = pl.program_id(1)
    hw = 2 * head_dim
    rows = q_ref.shape[0]
    hr = 2 * rows

    @pl.when(j == 0)
    def _():
        m_scr[...] = jnp.full_like(m_scr, NEG)
        acc_scr[...] = jnp.zeros_like(acc_scr)
        for h in range(n_heads):
            qq_scr[h] = _stack_maps(q_ref[:, h * hw:(h + 1) * hw], head_dim)

    def softmax_and_values(n_keys, v_of_head):
        m_prev = m_scr[...]
        s_tiles = [s_scr[:, t * LANES:(t + 1) * LANES] for t in range(max(n_keys // LANES, 1))]
        if n_keys < LANES:
            m_cur = jnp.max(s_scr[:, 0:n_keys], axis=-1, keepdims=True)
        else:
            m_cur = functools.reduce(jnp.maximum, s_tiles).max(axis=-1, keepdims=True)
        m_new = jnp.maximum(m_prev, m_cur)
        m_scr[...] = m_new
        alpha = jnp.exp2(m_prev - m_new)
        if n_keys < LANES:
            p = jnp.exp2(s_scr[:, 0:n_keys] - m_new[:, 0:n_keys]).astype(BF16)
        else:
            p = jnp.concatenate([jnp.exp2(st - m_new) for st in s_tiles], axis=1).astype(BF16)
        ones = jnp.ones((n_keys, hw), BF16)
        for h in range(n_heads):
            rs = slice(h * hr, (h + 1) * hr)
            v1 = jnp.concatenate([v_of_head(h), ones], axis=1)
            a = alpha[rs]
            acc_scr[h] = jnp.concatenate([a, a], axis=1) * acc_scr[h] + _dot(p[rs], v1)

    @pl.when(j < n_cache_tiles)
    def _():
        for h in range(n_heads):
            s_scr[h * hr:(h + 1) * hr, :] = _dot(qq_scr[h], ck_ref[h * hw:(h + 1) * hw, :].astype(BF16))
        softmax_and_values(ck_ref.shape[1], lambda h: cv_ref[:, h * hw:(h + 1) * hw].astype(BF16))

    @pl.when(j == n_cache_tiles)
    def _():
        n_new = nk_ref.shape[0]
        row = lax.broadcasted_iota(jnp.int32, (hr, n_new), 0)
        row = jnp.where(row >= rows, row - rows, row)
        col = lax.broadcasted_iota(jnp.int32, (hr, n_new), 1)
        visible = ((past_len + col) >> CHUNK_SHIFT) <= ((past_len + row) >> CHUNK_SHIFT)
        for h in range(n_heads):
            s = _dot_nt(qq_scr[h], nk_ref[:, h * hw:(h + 1) * hw])
            s_scr[h * hr:(h + 1) * hr, 0:n_new] = jnp.where(visible, s, NEG)
        softmax_and_values(n_new, lambda h: nv_ref[:, h * hw:(h + 1) * hw])
        lam = lam_ref[...]
        sub = sub_ref[...]
        for h in range(n_heads):
            acc = acc_scr[h]
            o = acc[:, :hw] * (1.0 / acc[:, hw:])
            o = o[:rows] - lam * o[rows:]
            o_ref[:, h * hw:(h + 1) * hw] = (_rms(o, sub) * out_scale).astype(o_ref.dtype)


def _attn_sample_call(lam, sub, q, cache_k, cache_v, new_k, new_v, *, n_batch, dec_seq, past_len, n_heads,
                      head_dim, tk, out_scale):
    d_model = q.shape[1]
    hw = 2 * head_dim
    n_tiles = past_len // tk
    last = n_tiles - 1
    cache_map = lambda b, j: (b * n_tiles + jnp.minimum(j, last), 0)
    cache_kt_map = lambda b, j: (b, jnp.minimum(j, last))
    return pl.pallas_call(
        functools.partial(_attn_sample_kernel, n_heads=n_heads, head_dim=head_dim, n_cache_tiles=n_tiles,
                          past_len=past_len, out_scale=out_scale),
        grid=(n_batch, n_tiles + 1),
        in_specs=[
            pl.BlockSpec((1, LANES), lambda b, j: (0, 0)),
            pl.BlockSpec((1, hw), lambda b, j: (0, 0)),
            pl.BlockSpec((dec_seq, d_model), lambda b, j: (b, 0)),
            pl.BlockSpec((d_model, tk), cache_kt_map),
            pl.BlockSpec((tk, d_model), cache_map),
            pl.BlockSpec((dec_seq, d_model), lambda b, j: (b, 0)),
            pl.BlockSpec((dec_seq, d_model), lambda b, j: (b, 0)),
        ],
        out_specs=pl.BlockSpec((dec_seq, d_model), lambda b, j: (b, 0)),
        out_shape=jax.ShapeDtypeStruct(q.shape, BF16),
        scratch_shapes=[pltpu.VMEM((n_heads, 2 * dec_seq, hw), BF16),
                        pltpu.VMEM((n_heads * 2 * dec_seq, tk), F32),
                        pltpu.VMEM((n_heads * 2 * dec_seq, LANES), F32),
                        pltpu.VMEM((n_heads, 2 * dec_seq, 2 * hw), F32)],
        compiler_params=_params("parallel", "arbitrary"),
        name="attn_sample",
    )(lam, sub, q, cache_k, cache_v, new_k, new_v)


def _rope_tables(pos, head_dim, reps):
    half = head_dim // 2
    inv = ROPE_THETA ** (-jnp.arange(half, dtype=F32) / half)
    ang = pos.astype(F32)[:, None] * inv[None, :]
    cos = jnp.cos(ang)
    sin = jnp.sin(ang)
    n_groups = LANES // head_dim
    cos_t = jnp.tile(cos, (reps, 2 * n_groups))
    sin_t = jnp.tile(jnp.concatenate([-sin, sin], axis=-1), (reps, n_groups))
    return cos_t, sin_t


def _row_tile(m_rows, cap):
    t = cap
    while m_rows % t:
        t //= 2
    return t


def _trunk(x, pos, n_seq, h0, cache, p, prompt):
    m_rows, d_model = x.shape
    slen = m_rows // n_seq
    n_a = p['norm_ssm'].shape[0]
    depth = n_a + p['norm_attn'].shape[0]
    head_dim = p['k_norm'].shape[0]
    hw = 2 * head_dim
    n_heads = d_model // hw
    k_cols = n_heads * hw
    tm = _row_tile(m_rows, 512)
    tm_big = _row_tile(m_rows, 1024)
    d_ff = p['w_ffn_out'][0].shape[0]
    tf = 512 if d_ff % 512 == 0 else 256
    row = lambda a: a.astype(F32).reshape(1, -1)

    if prompt:
        seg_len = 32
        rope_reps = 1
    else:
        seg_len = slen
        rope_reps = tm // slen
    cos_t, sin_t = _rope_tables(pos, head_dim, rope_reps)

    new_h = []
    k_f32 = v_f32 = k_b = v_b = None
    for layer in range(depth):
        if layer < n_a:
            i = layer
            tabs = _ssm_tables(p['ssm_lam_re'][i], p['ssm_lam_im'][i], p['ssm_log_step'][i], p['ssm_b_re'][i],
                               p['ssm_b_im'][i], p['ssm_c_re'][i], p['ssm_c_im'][i], p['ssm_d'][i], seg_len)
            n_cb, _, sw = tabs['a_re'].shape
            if prompt:
                t_rows = SSM_SEG * seg_len
                gy, h_re, h_im = _s5_call(x, row(p['norm_ssm'][i]), tabs, None, seg_len=seg_len,
                                          chunks_per_seq=slen // t_rows)
                last = (jnp.arange(n_seq) + 1) * (slen // seg_len) - 1
                h_re, h_im = h_re[:, last], h_im[:, last]
            else:
                to_cb = lambda a: a.astype(F32).reshape(n_seq, n_cb, sw).transpose(1, 0, 2)
                gy, h_re, h_im = _s5_call(x, row(p['norm_ssm'][i]), tabs, (to_cb(h0[0][i]), to_cb(h0[1][i])),
                                          seg_len=seg_len, chunks_per_seq=1)
            from_cb = lambda a: a.transpose(1, 0, 2).reshape(n_seq, -1)
            new_h.append((from_cb(h_re), from_cb(h_im)))
            x = _glu_call(gy, p['w_glu'][i], x, tm=tm_big, tn=512)
        else:
            if layer == n_a:
                g_kv = row(p['norm_kv'])
                hn_k = jnp.tile(row(p['k_norm']), (1, LANES // head_dim))
                k_f32, k_b = _proj_call(x, g_kv, p['w_kv'], 0, k_cols,
                                        ("cols_f32" if prompt else "rows_f32", "rows_bf16"), tm=tm, tn=512,
                                        rope=(hn_k, cos_t, sin_t), head_dim=head_dim, name="k_proj", seq=slen)
                if prompt:
                    k_f32 = k_f32.reshape(n_seq, n_heads, 2, head_dim, slen).transpose(0, 4, 1, 2, 3)
                else:
                    k_f32 = k_f32.reshape(n_seq, slen, n_heads, 2, head_dim)
                v_f32, v_b = _proj_call(x, g_kv, p['w_kv'], k_cols, p['w_kv'].shape[1] - k_cols,
                                        ("rows_f32", "cols_bf16" if prompt else "rows_bf16"), tm=tm, tn=512,
                                        name="v_proj", seq=slen)
            j = layer - n_a
            lambda_init = 0.8 - 0.6 * math.exp(-0.3 * layer)
            hn_q = jnp.tile(row(p['q_norm'][j]), (1, LANES // head_dim))
            (q,) = _proj_call(x, row(p['norm_attn'][j]), p['w_q'][j], 0, k_cols,
                              ("cols_bf16" if prompt else "rows_bf16",), tm=tm, tn=512,
                              rope=(hn_q, cos_t, sin_t), scale=head_dim ** -0.5 * LOG2_E, head_dim=head_dim,
                              name="q_proj", seq=slen)
            lam = (jnp.exp(jnp.sum(p['lambda_q1'][j].astype(F32) * p['lambda_k1'][j].astype(F32)))
                   - jnp.exp(jnp.sum(p['lambda_q2'][j].astype(F32) * p['lambda_k2'][j].astype(F32)))
                   + lambda_init)
            sub = row(p['subln'][j])
            if prompt:
                o = _attn_prompt_call(jnp.full((1, tm), lam, F32), sub.reshape(-1, 1), q, k_b, v_b, n_batch=n_seq,
                                      seq=slen, n_heads=n_heads, head_dim=head_dim, out_scale=1.0 - lambda_init)
            else:
                lam = jnp.full((1, LANES), lam, F32)
                cache_k, cache_v = cache
                past_len = cache_k.shape[1]
                cache_kt = cache_k.transpose(0, 2, 3, 4, 1).reshape(n_seq * d_model, past_len)
                o = _attn_sample_call(lam, sub, q, cache_kt,
                                      cache_v.reshape(n_seq * past_len, d_model), k_b, v_b, n_batch=n_seq,
                                      dec_seq=slen, past_len=past_len, n_heads=n_heads, head_dim=head_dim,
                                      tk=512, out_scale=1.0 - lambda_init)
            x = _resmm_call(o, p['w_o'][j], x, tm=tm_big, tn=512)
        x = _ffn_call(x, row(p['norm_ffn'][layer]), p['w_ffn_in'][layer], p['w_ffn_out'][layer], tm=tm, tf=tf)
    return x, new_h, k_f32, v_f32


def kernel(x_prompt, x_sample, state_ssm_re, state_ssm_im, cache_k, cache_v, norm_ssm, ssm_lam_re, ssm_lam_im, ssm_log_step, ssm_b_re, ssm_b_im, ssm_c_re, ssm_c_im, ssm_d, w_glu, norm_kv, w_kv, k_norm, norm_attn, w_q, q_norm, lambda_q1, lambda_k1, lambda_q2, lambda_k2, subln, w_o, norm_ffn, w_ffn_in, w_ffn_out):
    n_b, seq, d_model = x_prompt.shape
    n_db, dec_seq, _ = x_sample.shape
    past_len = cache_k.shape[1]
    head_dim = k_norm.shape[0]
    n_heads = d_model // (2 * head_dim)
    n_groups, n_state = ssm_lam_re.shape[1:]
    assert ssm_b_re.shape[3] * n_groups == d_model and LANES % ssm_b_re.shape[3] == 0
    assert 2 * head_dim == LANES and seq % (SSM_SEG * 32) == 0 and n_db % SSM_SEG == 0
    assert past_len % CHUNK == 0 and dec_seq % 16 == 0

    p = dict(norm_ssm=norm_ssm, ssm_lam_re=ssm_lam_re, ssm_lam_im=ssm_lam_im, ssm_log_step=ssm_log_step,
             ssm_b_re=ssm_b_re, ssm_b_im=ssm_b_im, ssm_c_re=ssm_c_re, ssm_c_im=ssm_c_im, ssm_d=ssm_d,
             w_glu=w_glu.astype(BF16), norm_kv=norm_kv, w_kv=w_kv.astype(BF16), k_norm=k_norm,
             norm_attn=norm_attn, w_q=w_q.astype(BF16), q_norm=q_norm, lambda_q1=lambda_q1,
             lambda_k1=lambda_k1, lambda_q2=lambda_q2, lambda_k2=lambda_k2, subln=subln,
             w_o=w_o.astype(BF16), norm_ffn=norm_ffn, w_ffn_in=w_ffn_in.astype(BF16),
             w_ffn_out=w_ffn_out.astype(BF16))

    pos_p = jnp.arange(seq, dtype=jnp.int32)
    y_p, h_p, k_p, v_p = _trunk(x_prompt.reshape(n_b * seq, d_model), pos_p, n_b, None, None, p, True)
    pos_s = past_len + jnp.arange(dec_seq, dtype=jnp.int32)
    y_s, h_s, k_s, v_s = _trunk(x_sample.reshape(n_db * dec_seq, d_model), pos_s, n_db,
                                (state_ssm_re, state_ssm_im), (cache_k, cache_v), p, False)

    def states(hs, n_seq, part):
        return jnp.stack([h[part].reshape(n_seq, n_groups, n_state) for h in hs])

    return (y_p.reshape(n_b, seq, d_model), y_s.reshape(n_db, dec_seq, d_model),
            states(h_p, n_b, 0), states(h_p, n_b, 1),
            k_p, v_p.reshape(n_b, seq, n_heads, 2 * head_dim),
            states(h_s, n_db, 0), states(h_s, n_db, 1),
            k_s, v_s.reshape(n_db, dec_seq, n_heads, 2 * head_dim))
```

```python
import functools
import math

import jax
import jax.numpy as jnp
from jax import lax
from jax.experimental import pallas as pl
from jax.experimental.pallas import tpu as pltpu

CHUNK = 64
CHUNK_SHIFT = CHUNK.bit_length() - 1
EPS = 1e-6
ROPE_THETA = 10000.0
LANES = 128
SUBLANES = 8
SSM_SEG = SUBLANES
VMEM_LIMIT = 56 * 1024 * 1024

F32 = jnp.float32
BF16 = jnp.bfloat16
NEG = -1e30
LOG2_E = math.log2(math.e)


def _dot(a, b):
    return jnp.dot(a, b, preferred_element_type=F32)


def _dot_nt(a, b):
    return lax.dot_general(a, b, (((1,), (1,)), ((), ())), preferred_element_type=F32)


def _rms(x, g):
    return x * lax.rsqrt(jnp.mean(x * x, axis=-1, keepdims=True) + EPS) * g


def _split_bf16(x):
    hi = x.astype(BF16)
    lo = (x - hi.astype(F32)).astype(BF16)
    return hi, lo


def _params(*sem):
    return pltpu.CompilerParams(dimension_semantics=sem, vmem_limit_bytes=VMEM_LIMIT)


def _cmul(ar, ai, br, bi):
    return ar * br - ai * bi, ar * bi + ai * br


def _cpow(ar, ai, n):
    rr, ri = None, None
    br, bi = ar, ai
    while n:
        if n & 1:
            rr, ri = (br, bi) if rr is None else _cmul(rr, ri, br, bi)
        n >>= 1
        if n:
            br, bi = _cmul(br, bi, br, bi)
    return rr, ri


def _s5_kernel(*refs, chain, seg_len, chunks_per_seq, n_cb, cw):
    if chain:
        (x_ref, g_ref, p_ref, pt_ref, b_ref, c_ref, are_ref, aim_ref, d_ref,
         gy_ref, hre_ref, him_ref, u_scr, bu_scr, gyp_scr, cre_scr, cim_scr) = refs
    else:
        (x_ref, g_ref, p_ref, pt_ref, b_ref, c_ref, are_ref, aim_ref, d_ref,
         h0re_ref, h0im_ref, gy_ref, hre_ref, him_ref, u_scr, bu_scr, gyp_scr) = refs
    sw = cw // 2

    u = _rms(x_ref[...], g_ref[...]).astype(BF16)
    up = _dot(p_ref[...], u).astype(BF16)
    for cb in range(n_cb):
        u_scr[cb] = up[:, cb * LANES:(cb + 1) * LANES]

    if chain:
        @pl.when(pl.program_id(0) % chunks_per_seq == 0)
        def _():
            cre_scr[...] = jnp.zeros_like(cre_scr)
            cim_scr[...] = jnp.zeros_like(cim_scr)

    row_id = lax.broadcasted_iota(jnp.int32, (SSM_SEG, sw), 0)

    def colblock(cb, carry):
        ub = u_scr[cb]
        bu_scr[...] = _dot(ub, b_ref[cb])
        are = are_ref[cb]
        aim = aim_ref[cb]
        if chain:
            hre = jnp.zeros((SSM_SEG, sw), F32)
            him = jnp.zeros((SSM_SEG, sw), F32)
        else:
            hre = h0re_ref[cb]
            him = h0im_ref[cb]
        for k in range(seg_len):
            r0 = k * SSM_SEG
            nre = are * hre - aim * him + bu_scr[r0:r0 + SSM_SEG, 0:sw]
            nim = are * him + aim * hre + bu_scr[r0:r0 + SSM_SEG, sw:cw]
            hre, him = nre, nim
            bu_scr[r0:r0 + SSM_SEG, 0:sw] = hre
            bu_scr[r0:r0 + SSM_SEG, sw:cw] = him

        if chain:
            pre, pim = _cpow(are[0:1], aim[0:1], seg_len)
            cur_re = cre_scr[cb][SSM_SEG - 1:SSM_SEG]
            cur_im = cim_scr[cb][SSM_SEG - 1:SSM_SEG]
            prev_re = jnp.zeros((SSM_SEG, sw), F32)
            prev_im = jnp.zeros((SSM_SEG, sw), F32)
            end_re = jnp.zeros((SSM_SEG, sw), F32)
            end_im = jnp.zeros((SSM_SEG, sw), F32)
            for j in range(SSM_SEG):
                prev_re = jnp.where(row_id == j, cur_re, prev_re)
                prev_im = jnp.where(row_id == j, cur_im, prev_im)
                mre, mim = _cmul(pre, pim, cur_re, cur_im)
                cur_re = mre + hre[j:j + 1]
                cur_im = mim + him[j:j + 1]
                end_re = jnp.where(row_id == j, cur_re, end_re)
                end_im = jnp.where(row_id == j, cur_im, end_im)
            cre_scr[cb] = end_re
            cim_scr[cb] = end_im
            hre_ref[cb] = end_re
            him_ref[cb] = end_im
            qre, qim = _cmul(are, aim, prev_re, prev_im)
            for k in range(seg_len):
                r0 = k * SSM_SEG
                bu_scr[r0:r0 + SSM_SEG, 0:sw] = bu_scr[r0:r0 + SSM_SEG, 0:sw] + qre
                bu_scr[r0:r0 + SSM_SEG, sw:cw] = bu_scr[r0:r0 + SSM_SEG, sw:cw] + qim
                if k + 1 < seg_len:
                    qre, qim = _cmul(are, aim, qre, qim)
        else:
            hre_ref[cb] = hre
            him_ref[cb] = him

        y = _dot(bu_scr[...].astype(BF16), c_ref[cb])
        y = y + d_ref[cb] * ub.astype(F32)
        gyp_scr[cb] = jax.nn.gelu(y, approximate=True).astype(BF16)
        return carry

    lax.fori_loop(0, n_cb, colblock, 0)

    perm_t = pt_ref[...]
    for cb in range(n_cb):
        gy_ref[:, cb * LANES:(cb + 1) * LANES] = _dot(perm_t, gyp_scr[cb]).astype(BF16)


def _s5_call(x, g, ssm, h0, *, seg_len, chunks_per_seq):
    m_rows, d_model = x.shape
    chain = h0 is None
    t_rows = SSM_SEG * seg_len
    n_chunks = m_rows // t_rows
    n_cb, _, cw = ssm['b'].shape
    sw = cw // 2
    full3 = lambda i: (0, 0, 0)
    full2 = lambda i: (0, 0)
    in_specs = [
        pl.BlockSpec((t_rows, d_model), lambda i: (i, 0)),
        pl.BlockSpec((1, d_model), full2),
        pl.BlockSpec((t_rows, t_rows), full2),
        pl.BlockSpec((t_rows, t_rows), full2),
        pl.BlockSpec((n_cb, LANES, cw), full3),
        pl.BlockSpec((n_cb, cw, LANES), full3),
        pl.BlockSpec((n_cb, SSM_SEG, sw), full3),
        pl.BlockSpec((n_cb, SSM_SEG, sw), full3),
        pl.BlockSpec((n_cb, 1, LANES), full3),
    ]
    args = [x, g, ssm['perm'], ssm['perm_t'], ssm['b'], ssm['c'], ssm['a_re'], ssm['a_im'], ssm['d']]
    if not chain:
        in_specs += [pl.BlockSpec((n_cb, SSM_SEG, sw), lambda i: (0, i, 0))] * 2
        args += [h0[0], h0[1]]
    scratch = [
        pltpu.VMEM((n_cb, t_rows, LANES), BF16),
        pltpu.VMEM((t_rows, cw), F32),
        pltpu.VMEM((n_cb, t_rows, LANES), BF16),
    ]
    if chain:
        scratch += [pltpu.VMEM((n_cb, SSM_SEG, sw), F32)] * 2
    state_shape = jax.ShapeDtypeStruct((n_cb, n_chunks * SSM_SEG, sw), F32)
    return pl.pallas_call(
        functools.partial(_s5_kernel, chain=chain, seg_len=seg_len, chunks_per_seq=chunks_per_seq,
                          n_cb=n_cb, cw=cw),
        grid=(n_chunks,),
        in_specs=in_specs,
        out_specs=[
            pl.BlockSpec((t_rows, d_model), lambda i: (i, 0)),
            pl.BlockSpec((n_cb, SSM_SEG, sw), lambda i: (0, i, 0)),
            pl.BlockSpec((n_cb, SSM_SEG, sw), lambda i: (0, i, 0)),
        ],
        out_shape=[jax.ShapeDtypeStruct((m_rows, d_model), BF16), state_shape, state_shape],
        scratch_shapes=scratch,
        compiler_params=_params("arbitrary"),
        name="s5_chain" if chain else "s5_step",
    )(*args)


def _ssm_tables(lam_re, lam_im, log_step, b_re, b_im, c_re, c_im, d, seg_len):
    n_groups, n_state, n_chan = b_re.shape
    gpb = LANES // n_chan
    n_cb = n_groups // gpb
    sw = gpb * n_state
    lam_re = lam_re.astype(F32)
    lam_im = lam_im.astype(F32)
    dt = jnp.exp(log_step.astype(F32))[:, None]
    mag = jnp.exp(lam_re * dt)
    a_re = mag * jnp.cos(lam_im * dt)
    a_im = mag * jnp.sin(lam_im * dt)
    den = lam_re * lam_re + lam_im * lam_im
    q_re = ((a_re - 1.0) * lam_re + a_im * lam_im) / den
    q_im = (a_im * lam_re - (a_re - 1.0) * lam_im) / den
    bb_re = q_re[..., None] * b_re.astype(F32) - q_im[..., None] * b_im.astype(F32)
    bb_im = q_re[..., None] * b_im.astype(F32) + q_im[..., None] * b_re.astype(F32)
    eye = jnp.eye(gpb, dtype=F32)

    def bmat(bb):
        t = bb.reshape(n_cb, gpb, n_state, n_chan)
        t = jnp.einsum('cgnp,gh->cgphn', t, eye)
        return t.reshape(n_cb, gpb * n_chan, sw)

    def cmat(cc):
        t = cc.astype(F32).reshape(n_cb, gpb, n_chan, n_state)
        t = jnp.einsum('cgpn,gh->cgnhp', t, eye)
        return t.reshape(n_cb, sw, gpb * n_chan)

    b_full = jnp.concatenate([bmat(bb_re), bmat(bb_im)], axis=-1).astype(BF16)
    c_full = jnp.concatenate([cmat(c_re), -cmat(c_im)], axis=1).astype(BF16)
    rep = lambda a: jnp.broadcast_to(a.reshape(n_cb, 1, sw), (n_cb, SSM_SEG, sw))
    t_rows = SSM_SEG * seg_len
    r = jnp.arange(t_rows)
    src = (r % SSM_SEG) * seg_len + r // SSM_SEG
    perm = (src[:, None] == jnp.arange(t_rows)[None, :]).astype(BF16)
    return dict(b=b_full, c=c_full, a_re=rep(a_re), a_im=rep(a_im),
                d=d.astype(F32).reshape(n_cb, 1, LANES), perm=perm, perm_t=perm.T)


def _glu_kernel(a_ref, wv_ref, wg_ref, x_ref, o_ref):
    a = a_ref[...]
    zv = _dot(a, wv_ref[...])
    zg = _dot(a, wg_ref[...])
    o_ref[...] = x_ref[...] + zv * jax.nn.sigmoid(zg)


def _glu_call(a, w, x, *, tm, tn):
    m_rows, d_in = a.shape
    d_out = x.shape[1]
    nj = d_out // tn
    return pl.pallas_call(
        _glu_kernel,
        grid=(m_rows // tm, nj),
        in_specs=[
            pl.BlockSpec((tm, d_in), lambda i, j: (i, 0)),
            pl.BlockSpec((d_in, tn), lambda i, j: (0, j)),
            pl.BlockSpec((d_in, tn), lambda i, j: (0, j + nj)),
            pl.BlockSpec((tm, tn), lambda i, j: (i, j)),
        ],
        out_specs=pl.BlockSpec((tm, tn), lambda i, j: (i, j)),
        out_shape=jax.ShapeDtypeStruct((m_rows, d_out), F32),
        compiler_params=_params("parallel", "arbitrary"),
        name="glu",
    )(a, w, w, x)


def _ffn_kernel(x_ref, g_ref, wa_ref, wb_ref, wo_ref, o_ref, xn_scr):
    @pl.when(pl.program_id(1) == 0)
    def _():
        x = x_ref[...]
        xn_scr[...] = _rms(x, g_ref[...]).astype(BF16)
        o_ref[...] = x

    xn = xn_scr[...]
    a = _dot(xn, wa_ref[...])
    b = _dot(xn, wb_ref[...])
    h = (a * jax.nn.sigmoid(a) * b).astype(BF16)
    o_ref[...] += _dot(h, wo_ref[...])


def _ffn_call(x, g, w_in, w_out, *, tm, tf):
    m_rows, d_model = x.shape
    d_ff = w_out.shape[0]
    nj = d_ff // tf
    return pl.pallas_call(
        _ffn_kernel,
        grid=(m_rows // tm, nj),
        in_specs=[
            pl.BlockSpec((tm, d_model), lambda i, j: (i, 0)),
            pl.BlockSpec((1, d_model), lambda i, j: (0, 0)),
            pl.BlockSpec((d_model, tf), lambda i, j: (0, j)),
            pl.BlockSpec((d_model, tf), lambda i, j: (0, j + nj)),
            pl.BlockSpec((tf, d_model), lambda i, j: (j, 0)),
        ],
        out_specs=pl.BlockSpec((tm, d_model), lambda i, j: (i, 0)),
        out_shape=jax.ShapeDtypeStruct((m_rows, d_model), F32),
        scratch_shapes=[pltpu.VMEM((tm, d_model), BF16)],
        compiler_params=_params("parallel", "arbitrary"),
        name="ffn",
    )(x, g, w_in, w_in, w_out)


def _proj_kernel(*refs, head_norm, scale, outs, head_dim):
    n_out = len(outs)
    if head_norm:
        x_ref, g_ref, w_ref, hn_ref, cos_ref, sin_ref = refs[:6]
        out_refs = refs[6:6 + n_out]
    else:
        x_ref, g_ref, w_ref = refs[:3]
        out_refs = refs[3:3 + n_out]
    xn_scr = refs[-1]

    @pl.when(pl.program_id(1) == 0)
    def _():
        xn_scr[...] = _rms(x_ref[...], g_ref[...]).astype(BF16)

    y = _dot(xn_scr[...], w_ref[...])
    tm, tn = y.shape
    if head_norm:
        half = head_dim // 2
        hd_shift = head_dim.bit_length() - 1
        gi = lax.broadcasted_iota(jnp.int32, (LANES, LANES), 0) >> hd_shift
        gj = lax.broadcasted_iota(jnp.int32, (LANES, LANES), 1) >> hd_shift
        gmat = (gi == gj).astype(BF16)
        lane = lax.broadcasted_iota(jnp.int32, (tm, LANES), 1)
        first_half = (lane & (head_dim - 1)) < half
        cos = cos_ref[...]
        sin = sin_ref[...]
        hn = hn_ref[...]
    need_t = any(kind.startswith("cols") for kind in outs)
    for s in range(tn // LANES):
        cs = slice(s * LANES, (s + 1) * LANES)
        r = y[:, cs]
        if head_norm:
            sq_hi, sq_lo = _split_bf16(r * r)
            ss = _dot(sq_hi, gmat) + _dot(sq_lo, gmat)
            yn = r * lax.rsqrt(ss * (1.0 / head_dim) + EPS) * hn
            partner = jnp.where(first_half, pltpu.roll(yn, LANES - half, 1), pltpu.roll(yn, half, 1))
            r = yn * cos + partner * sin
            if scale != 1.0:
                r = r * scale
        rt = r.T if need_t else None
        for kind, o_ref in zip(outs, out_refs):
            if kind == "cols_f32":
                o_ref[cs, :] = rt
            elif kind == "cols_bf16":
                o_ref[0, 0, cs, :] = rt.astype(BF16)
            else:
                o_ref[:, cs] = r.astype(o_ref.dtype)


def _proj_call(x, g, w, col0, n_cols, outs, *, tm, tn, rope=None, scale=1.0, head_dim=64, name="proj", seq=None):
    m_rows, d_model = x.shape
    nj = n_cols // tn
    j0 = col0 // tn
    head_norm = rope is not None
    out_specs, out_shape = [], []
    for kind in outs:
        if kind == "cols_f32":
            tiles = seq // tm
            out_specs.append(pl.BlockSpec((tn, tm), lambda i, j: ((i // tiles) * nj + j, i % tiles)))
            out_shape.append(jax.ShapeDtypeStruct((m_rows // seq * n_cols, seq), F32))
        elif kind == "cols_bf16":
            tiles = seq // tm
            out_specs.append(pl.BlockSpec((1, 1, tn, tm), lambda i, j: (i // tiles, i % tiles, j, 0)))
            out_shape.append(jax.ShapeDtypeStruct((m_rows // seq, tiles, n_cols, tm), BF16))
        else:
            out_specs.append(pl.BlockSpec((tm, tn), lambda i, j: (i, j)))
            out_shape.append(jax.ShapeDtypeStruct((m_rows, n_cols), F32 if kind == "rows_f32" else BF16))
    in_specs = [
        pl.BlockSpec((tm, d_model), lambda i, j: (i, 0)),
        pl.BlockSpec((1, d_model), lambda i, j: (0, 0)),
        pl.BlockSpec((d_model, tn), lambda i, j: (0, j + j0)),
    ]
    args = [x, g, w]
    if head_norm:
        hn, cos, sin = rope
        nrep = cos.shape[0] // tm
        in_specs += [
            pl.BlockSpec((1, LANES), lambda i, j: (0, 0)),
            pl.BlockSpec((tm, LANES), lambda i, j: (i % nrep, 0)),
            pl.BlockSpec((tm, LANES), lambda i, j: (i % nrep, 0)),
        ]
        args += [hn, cos, sin]
    return pl.pallas_call(
        functools.partial(_proj_kernel, head_norm=head_norm, scale=scale, outs=tuple(outs), head_dim=head_dim),
        grid=(m_rows // tm, nj),
        in_specs=in_specs,
        out_specs=out_specs,
        out_shape=out_shape,
        scratch_shapes=[pltpu.VMEM((tm, d_model), BF16)],
        compiler_params=_params("parallel", "arbitrary"),
        name=name,
    )(*args)


def _resmm_kernel(a_ref, w_ref, x_ref, o_ref):
    o_ref[...] = x_ref[...] + _dot(a_ref[...], w_ref[...])


def _resmm_call(a, w, x, *, tm, tn):
    m_rows, d_in = a.shape
    d_out = w.shape[1]
    return pl.pallas_call(
        _resmm_kernel,
        grid=(m_rows // tm, d_out // tn),
        in_specs=[
            pl.BlockSpec((tm, d_in), lambda i, j: (i, 0)),
            pl.BlockSpec((d_in, tn), lambda i, j: (0, j)),
            pl.BlockSpec((tm, tn), lambda i, j: (i, j)),
        ],
        out_specs=pl.BlockSpec((tm, tn), lambda i, j: (i, j)),
        out_shape=jax.ShapeDtypeStruct((m_rows, d_out), F32),
        compiler_params=_params("parallel", "arbitrary"),
        name="out_proj",
    )(a, w, x)


def _stack_maps(q, head_dim):
    lane = lax.broadcasted_iota(jnp.int32, q.shape, 1)
    zero = jnp.zeros_like(q)
    return jnp.concatenate([jnp.where(lane < head_dim, q, zero), jnp.where(lane >= head_dim, q, zero)], axis=0)


ONES_ROWS = 16


def _attn_prompt_kernel(lam_ref, sub_ref, qt_ref, k_ref, vt_ref, o_ref, acc_scr, st_scr, m_scr,
                        *, tq, head_dim, out_scale):
    qi = pl.program_id(2)
    hw = 2 * head_dim
    qt = qt_ref[0, 0]
    feat = lax.broadcasted_iota(jnp.int32, qt.shape, 0)
    zero = jnp.zeros_like(qt)
    qqt = jnp.concatenate([jnp.where(feat < head_dim, qt, zero), jnp.where(feat >= head_dim, qt, zero)], axis=1)
    acc_scr[...] = jnp.zeros_like(acc_scr)
    m_scr[...] = jnp.full_like(m_scr, NEG)
    ones = jnp.ones((ONES_ROWS, tq), BF16)

    def scores(kj, slot):
        start = pl.multiple_of(kj * tq, tq)
        st_scr[slot] = _dot(k_ref[pl.ds(start, tq), :], qqt)

    def softmax_values(kj, slot, masked):
        st = st_scr[slot]
        if masked:
            kpos = lax.broadcasted_iota(jnp.int32, st.shape, 0)
            qpos = lax.broadcasted_iota(jnp.int32, st.shape, 1)
            qpos = jnp.where(qpos >= tq, qpos - tq, qpos)
            st = jnp.where((kpos >> CHUNK_SHIFT) <= (qpos >> CHUNK_SHIFT), st, NEG)
        m_prev = m_scr[...]
        m_new = jnp.maximum(m_prev, jnp.max(st, axis=0, keepdims=True))
        m_scr[...] = m_new
        alpha = jnp.exp2(m_prev - m_new)
        pt = jnp.exp2(st - m_new).astype(BF16)
        vt1 = jnp.concatenate([vt_ref[0, kj], ones], axis=0)
        acc_scr[...] = alpha * acc_scr[...] + _dot(vt1, pt)

    scores(0, 0)

    def two_tiles(t, carry):
        scores(2 * t + 1, 1)
        softmax_values(2 * t, 0, False)
        scores(2 * t + 2, 0)
        softmax_values(2 * t + 1, 1, False)
        return carry

    lax.fori_loop(0, qi // 2, two_tiles, 0)

    @pl.when(qi % 2 == 0)
    def _():
        softmax_values(qi, 0, True)

    @pl.when(qi % 2 == 1)
    def _():
        scores(qi, 1)
        softmax_values(qi - 1, 0, False)
        softmax_values(qi, 1, True)

    acc = acc_scr[...]
    ot = acc[:hw] * (1.0 / acc[hw:hw + 1])
    ot = ot[:, :tq] - lam_ref[...] * ot[:, tq:]
    ot = ot * lax.rsqrt(jnp.mean(ot * ot, axis=0, keepdims=True) + EPS) * (sub_ref[...] * out_scale)
    o_ref[...] = ot.T.astype(o_ref.dtype)


def _attn_prompt_call(lam, sub, qt, k, vt, *, n_batch, seq, n_heads, head_dim, out_scale):
    tq = qt.shape[3]
    nq = seq // tq
    hw = 2 * head_dim
    return pl.pallas_call(
        functools.partial(_attn_prompt_kernel, tq=tq, head_dim=head_dim, out_scale=out_scale),
        grid=(n_batch, n_heads, nq),
        in_specs=[
            pl.BlockSpec((1, tq), lambda b, h, i: (0, 0)),
            pl.BlockSpec((hw, 1), lambda b, h, i: (0, 0)),
            pl.BlockSpec((1, 1, hw, tq), lambda b, h, i: (b, i, h, 0)),
            pl.BlockSpec((seq, hw), lambda b, h, i: (b, h)),
            pl.BlockSpec((1, nq, hw, tq), lambda b, h, i: (b, 0, h, 0)),
        ],
        out_specs=pl.BlockSpec((tq, hw), lambda b, h, i: (b * nq + i, h)),
        out_shape=jax.ShapeDtypeStruct(k.shape, BF16),
        scratch_shapes=[pltpu.VMEM((hw + ONES_ROWS, 2 * tq), F32), pltpu.VMEM((2, tq, 2 * tq), F32),
                        pltpu.VMEM((1, 2 * tq), F32)],
        compiler_params=_params("parallel", "parallel", "arbitrary"),
        name="attn_prompt",
    )(lam, sub, qt, k, vt)


def _attn_sample_kernel(lam_ref, sub_ref, q_ref, ck_ref, cv_ref, nk_ref, nv_ref, o_ref, qq_scr, s_scr, m_scr,
                        acc_scr, *, n_heads, head_dim, n_cache_tiles, past_len, out_scale):
    j = pl.program_id(1)
    hw = 2 * head_dim
    rows = q_ref.shape[0]
    hr = 2 * rows

    @pl.when(j == 0)
    def _():
        m_scr[...] = jnp.full_like(m_scr, NEG)
        acc_scr[...] = jnp.zeros_like(acc_scr)
        for h in range(n_heads):
            qq_scr[h] = _stack_maps(q_ref[:, h * hw:(h + 1) * hw], head_dim)

    def softmax_and_values(n_keys, v_of_head):
        m_prev = m_scr[...]
        s_tiles = [s_scr[:, t * LANES:(t + 1) * LANES] for t in range(max(n_keys // LANES, 1))]
        if n_keys < LANES:
            m_cur = jnp.max(s_scr[:, 0:n_keys], axis=-1, keepdims=True)
        else:
            m_cur = functools.reduce(jnp.maximum, s_tiles).max(axis=-1, keepdims=True)
        m_new = jnp.maximum(m_prev, m_cur)
        m_scr[...] = m_new
        alpha = jnp.exp2(m_prev - m_new)
        if n_keys < LANES:
            p = jnp.exp2(s_scr[:, 0:n_keys] - m_new[:, 0:n_keys]).astype(BF16)
        else:
            p = jnp.concatenate([jnp.exp2(st - m_new) for st in s_tiles], axis=1).astype(BF16)
        ones = jnp.ones((n_keys, hw), BF16)
        for h in range(n_heads):
            rs = slice(h * hr, (h + 1) * hr)
            v1 = jnp.concatenate([v_of_head(h), ones], axis=1)
            a = alpha[rs]
            acc_scr[h] = jnp.concatenate([a, a], axis=1) * acc_scr[h] + _dot(p[rs], v1)

    @pl.when(j < n_cache_tiles)
    def _():
        for h in range(n_heads):
            s_scr[h * hr:(h + 1) * hr, :] = _dot(qq_scr[h], ck_ref[h * hw:(h + 1) * hw, :].astype(BF16))
        n_keys = ck_ref.shape[1]
        softmax_and_values(n_keys, lambda h: cv_ref[pl.ds(h, n_keys, stride=n_heads), :].astype(BF16))

    @pl.when(j == n_cache_tiles)
    def _():
        n_new = nk_ref.shape[0]
        row = lax.broadcasted_iota(jnp.int32, (hr, n_new), 0)
        row = jnp.where(row >= rows, row - rows, row)
        col = lax.broadcasted_iota(jnp.int32, (hr, n_new), 1)
        visible = ((past_len + col) >> CHUNK_SHIFT) <= ((past_len + row) >> CHUNK_SHIFT)
        for h in range(n_heads):
            s = _dot_nt(qq_scr[h], nk_ref[:, h * hw:(h + 1) * hw])
            s_scr[h * hr:(h + 1) * hr, 0:n_new] = jnp.where(visible, s, NEG)
        softmax_and_values(n_new, lambda h: nv_ref[:, h * hw:(h + 1) * hw])
        lam = lam_ref[...]
        sub = sub_ref[...]
        for h in range(n_heads):
            acc = acc_scr[h]
            o = acc[:, :hw] * (1.0 / acc[:, hw:])
            o = o[:rows] - lam * o[rows:]
            o_ref[:, h * hw:(h + 1) * hw] = (_rms(o, sub) * out_scale).astype(o_ref.dtype)


def _attn_sample_call(lam, sub, q, cache_k, cache_v, new_k, new_v, *, n_batch, dec_seq, past_len, n_heads,
                      head_dim, tk, out_scale):
    d_model = q.shape[1]
    hw = 2 * head_dim
    n_tiles = past_len // tk
    last = n_tiles - 1
    cache_map = lambda b, j: (b * n_tiles + jnp.minimum(j, last), 0)
    cache_kt_map = lambda b, j: (b, jnp.minimum(j, last))
    return pl.pallas_call(
        functools.partial(_attn_sample_kernel, n_heads=n_heads, head_dim=head_dim, n_cache_tiles=n_tiles,
                          past_len=past_len, out_scale=out_scale),
        grid=(n_batch, n_tiles + 1),
        in_specs=[
            pl.BlockSpec((1, LANES), lambda b, j: (0, 0)),
            pl.BlockSpec((1, hw), lambda b, j: (0, 0)),
            pl.BlockSpec((dec_seq, d_model), lambda b, j: (b, 0)),
            pl.BlockSpec((d_model, tk), cache_kt_map),
            pl.BlockSpec((tk * n_heads, hw), cache_map),
            pl.BlockSpec((dec_seq, d_model), lambda b, j: (b, 0)),
            pl.BlockSpec((dec_seq, d_model), lambda b, j: (b, 0)),
        ],
        out_specs=pl.BlockSpec((dec_seq, d_model), lambda b, j: (b, 0)),
        out_shape=jax.ShapeDtypeStruct(q.shape, BF16),
        scratch_shapes=[pltpu.VMEM((n_heads, 2 * dec_seq, hw), BF16),
                        pltpu.VMEM((n_heads * 2 * dec_seq, tk), F32),
                        pltpu.VMEM((n_heads * 2 * dec_seq, LANES), F32),
                        pltpu.VMEM((n_heads, 2 * dec_seq, 2 * hw), F32)],
        compiler_params=_params("parallel", "arbitrary"),
        name="attn_sample",
    )(lam, sub, q, cache_k, cache_v, new_k, new_v)


def _rope_tables(pos, head_dim, reps):
    half = head_dim // 2
    inv = ROPE_THETA ** (-jnp.arange(half, dtype=F32) / half)
    ang = pos.astype(F32)[:, None] * inv[None, :]
    cos = jnp.cos(ang)
    sin = jnp.sin(ang)
    n_groups = LANES // head_dim
    cos_t = jnp.tile(cos, (reps, 2 * n_groups))
    sin_t = jnp.tile(jnp.concatenate([-sin, sin], axis=-1), (reps, n_groups))
    return cos_t, sin_t


def _row_tile(m_rows, cap):
    t = cap
    while m_rows % t:
        t //= 2
    return t


def _trunk(x, pos, n_seq, h0, cache, p, prompt):
    m_rows, d_model = x.shape
    slen = m_rows // n_seq
    n_a = p['norm_ssm'].shape[0]
    depth = n_a + p['norm_attn'].shape[0]
    head_dim = p['k_norm'].shape[0]
    hw = 2 * head_dim
    n_heads = d_model // hw
    k_cols = n_heads * hw
    tm = _row_tile(m_rows, 512)
    tm_big = _row_tile(m_rows, 1024)
    d_ff = p['w_ffn_out'][0].shape[0]
    tf = 512 if d_ff % 512 == 0 else 256
    row = lambda a: a.astype(F32).reshape(1, -1)

    if prompt:
        seg_len = 32
        rope_reps = 1
    else:
        seg_len = slen
        rope_reps = tm // slen
    cos_t, sin_t = _rope_tables(pos, head_dim, rope_reps)

    new_h = []
    k_f32 = v_f32 = k_b = v_b = None
    for layer in range(depth):
        if layer < n_a:
            i = layer
            tabs = _ssm_tables(p['ssm_lam_re'][i], p['ssm_lam_im'][i], p['ssm_log_step'][i], p['ssm_b_re'][i],
                               p['ssm_b_im'][i], p['ssm_c_re'][i], p['ssm_c_im'][i], p['ssm_d'][i], seg_len)
            n_cb, _, sw = tabs['a_re'].shape
            if prompt:
                t_rows = SSM_SEG * seg_len
                gy, h_re, h_im = _s5_call(x, row(p['norm_ssm'][i]), tabs, None, seg_len=seg_len,
                                          chunks_per_seq=slen // t_rows)
                last = (jnp.arange(n_seq) + 1) * (slen // seg_len) - 1
                h_re, h_im = h_re[:, last], h_im[:, last]
            else:
                to_cb = lambda a: a.astype(F32).reshape(n_seq, n_cb, sw).transpose(1, 0, 2)
                gy, h_re, h_im = _s5_call(x, row(p['norm_ssm'][i]), tabs, (to_cb(h0[0][i]), to_cb(h0[1][i])),
                                          seg_len=seg_len, chunks_per_seq=1)
            from_cb = lambda a: a.transpose(1, 0, 2).reshape(n_seq, -1)
            new_h.append((from_cb(h_re), from_cb(h_im)))
            x = _glu_call(gy, p['w_glu'][i], x, tm=tm_big, tn=512)
        else:
            if layer == n_a:
                g_kv = row(p['norm_kv'])
                hn_k = jnp.tile(row(p['k_norm']), (1, LANES // head_dim))
                k_f32, k_b = _proj_call(x, g_kv, p['w_kv'], 0, k_cols,
                                        ("cols_f32" if prompt else "rows_f32", "rows_bf16"), tm=tm, tn=512,
                                        rope=(hn_k, cos_t, sin_t), head_dim=head_dim, name="k_proj", seq=slen)
                if prompt:
                    k_f32 = k_f32.reshape(n_seq, n_heads, 2, head_dim, slen).transpose(0, 4, 1, 2, 3)
                else:
                    k_f32 = k_f32.reshape(n_seq, slen, n_heads, 2, head_dim)
                v_f32, v_b = _proj_call(x, g_kv, p['w_kv'], k_cols, p['w_kv'].shape[1] - k_cols,
                                        ("rows_f32", "cols_bf16" if prompt else "rows_bf16"), tm=tm, tn=512,
                                        name="v_proj", seq=slen)
            j = layer - n_a
            lambda_init = 0.8 - 0.6 * math.exp(-0.3 * layer)
            hn_q = jnp.tile(row(p['q_norm'][j]), (1, LANES // head_dim))
            (q,) = _proj_call(x, row(p['norm_attn'][j]), p['w_q'][j], 0, k_cols,
                              ("cols_bf16" if prompt else "rows_bf16",), tm=tm, tn=512,
                              rope=(hn_q, cos_t, sin_t), scale=head_dim ** -0.5 * LOG2_E, head_dim=head_dim,
                              name="q_proj", seq=slen)
            lam = (jnp.exp(jnp.sum(p['lambda_q1'][j].astype(F32) * p['lambda_k1'][j].astype(F32)))
                   - jnp.exp(jnp.sum(p['lambda_q2'][j].astype(F32) * p['lambda_k2'][j].astype(F32)))
                   + lambda_init)
            sub = row(p['subln'][j])
            if prompt:
                o = _attn_prompt_call(jnp.full((1, tm), lam, F32), sub.reshape(-1, 1), q, k_b, v_b, n_batch=n_seq,
                                      seq=slen, n_heads=n_heads, head_dim=head_dim, out_scale=1.0 - lambda_init)
            else:
                lam = jnp.full((1, LANES), lam, F32)
                cache_k, cache_v = cache
                past_len = cache_k.shape[1]
                cache_kt = cache_k.transpose(0, 2, 3, 4, 1).reshape(n_seq * d_model, past_len)
                o = _attn_sample_call(lam, sub, q, cache_kt,
                                      cache_v.reshape(n_seq * past_len * n_heads, hw), k_b, v_b, n_batch=n_seq,
                                      dec_seq=slen, past_len=past_len, n_heads=n_heads, head_dim=head_dim,
                                      tk=512, out_scale=1.0 - lambda_init)
            x = _resmm_call(o, p['w_o'][j], x, tm=tm_big, tn=512)
        x = _ffn_call(x, row(p['norm_ffn'][layer]), p['w_ffn_in'][layer], p['w_ffn_out'][layer], tm=tm, tf=tf)
    return x, new_h, k_f32, v_f32


def kernel(x_prompt, x_sample, state_ssm_re, state_ssm_im, cache_k, cache_v, norm_ssm, ssm_lam_re, ssm_lam_im, ssm_log_step, ssm_b_re, ssm_b_im, ssm_c_re, ssm_c_im, ssm_d, w_glu, norm_kv, w_kv, k_norm, norm_attn, w_q, q_norm, lambda_q1, lambda_k1, lambda_q2, lambda_k2, subln, w_o, norm_ffn, w_ffn_in, w_ffn_out):
    n_b, seq, d_model = x_prompt.shape
    n_db, dec_seq, _ = x_sample.shape
    past_len = cache_k.shape[1]
    head_dim = k_norm.shape[0]
    n_heads = d_model // (2 * head_dim)
    n_groups, n_state = ssm_lam_re.shape[1:]
    assert ssm_b_re.shape[3] * n_groups == d_model and LANES % ssm_b_re.shape[3] == 0
    assert 2 * head_dim == LANES and seq % (SSM_SEG * 32) == 0 and n_db % SSM_SEG == 0
    assert past_len % CHUNK == 0 and dec_seq % 16 == 0

    p = dict(norm_ssm=norm_ssm, ssm_lam_re=ssm_lam_re, ssm_lam_im=ssm_lam_im, ssm_log_step=ssm_log_step,
             ssm_b_re=ssm_b_re, ssm_b_im=ssm_b_im, ssm_c_re=ssm_c_re, ssm_c_im=ssm_c_im, ssm_d=ssm_d,
             w_glu=w_glu.astype(BF16), norm_kv=norm_kv, w_kv=w_kv.astype(BF16), k_norm=k_norm,
             norm_attn=norm_attn, w_q=w_q.astype(BF16), q_norm=q_norm, lambda_q1=lambda_q1,
             lambda_k1=lambda_k1, lambda_q2=lambda_q2, lambda_k2=lambda_k2, subln=subln,
             w_o=w_o.astype(BF16), norm_ffn=norm_ffn, w_ffn_in=w_ffn_in.astype(BF16),
             w_ffn_out=w_ffn_out.astype(BF16))

    pos_p = jnp.arange(seq, dtype=jnp.int32)
    y_p, h_p, k_p, v_p = _trunk(x_prompt.reshape(n_b * seq, d_model), pos_p, n_b, None, None, p, True)
    pos_s = past_len + jnp.arange(dec_seq, dtype=jnp.int32)
    y_s, h_s, k_s, v_s = _trunk(x_sample.reshape(n_db * dec_seq, d_model), pos_s, n_db,
                                (state_ssm_re, state_ssm_im), (cache_k, cache_v), p, False)

    def states(hs, n_seq, part):
        return jnp.stack([h[part].reshape(n_seq, n_groups, n_state) for h in hs])

    return (y_p.reshape(n_b, seq, d_model), y_s.reshape(n_db, dec_seq, d_model),
            states(h_p, n_b, 0), states(h_p, n_b, 1),
            k_p, v_p.reshape(n_b, seq, n_heads, 2 * head_dim),
            states(h_s, n_db, 0), states(h_s, n_db, 1),
            k_s, v_s.reshape(n_db, dec_seq, n_heads, 2 * head_dim))
```

```python
import functools
import math

import jax
import jax.numpy as jnp
from jax import lax
from jax.experimental import pallas as pl
from jax.experimental.pallas import tpu as pltpu

CHUNK = 64
CHUNK_SHIFT = CHUNK.bit_length() - 1
EPS = 1e-6
ROPE_THETA = 10000.0
LANES = 128
SUBLANES = 8
SSM_SEG = SUBLANES
VMEM_LIMIT = 56 * 1024 * 1024

F32 = jnp.float32
BF16 = jnp.bfloat16
NEG = -1e30
LOG2_E = math.log2(math.e)


def _dot(a, b):
    return jnp.dot(a, b, preferred_element_type=F32)


def _dot_nt(a, b):
    return lax.dot_general(a, b, (((1,), (1,)), ((), ())), preferred_element_type=F32)


def _rms(x, g):
    return x * lax.rsqrt(jnp.mean(x * x, axis=-1, keepdims=True) + EPS) * g


def _split_bf16(x):
    hi = x.astype(BF16)
    lo = (x - hi.astype(F32)).astype(BF16)
    return hi, lo


def _params(*sem):
    return pltpu.CompilerParams(dimension_semantics=sem, vmem_limit_bytes=VMEM_LIMIT)


def _cmul(ar, ai, br, bi):
    return ar * br - ai * bi, ar * bi + ai * br


def _cpow(ar, ai, n):
    rr, ri = None, None
    br, bi = ar, ai
    while n:
        if n & 1:
            rr, ri = (br, bi) if rr is None else _cmul(rr, ri, br, bi)
        n >>= 1
        if n:
            br, bi = _cmul(br, bi, br, bi)
    return rr, ri


def _s5_kernel(*refs, chain, seg_len, chunks_per_seq, n_cb, cw):
    if chain:
        (x_ref, g_ref, p_ref, pt_ref, b_ref, c_ref, are_ref, aim_ref, d_ref,
         gy_ref, hre_ref, him_ref, u_scr, bu_scr, gyp_scr, cre_scr, cim_scr) = refs
    else:
        (x_ref, g_ref, p_ref, pt_ref, b_ref, c_ref, are_ref, aim_ref, d_ref,
         h0re_ref, h0im_ref, gy_ref, hre_ref, him_ref, u_scr, bu_scr, gyp_scr) = refs
    sw = cw // 2

    u = _rms(x_ref[...], g_ref[...]).astype(BF16)
    up = _dot(p_ref[...], u).astype(BF16)
    for cb in range(n_cb):
        u_scr[cb] = up[:, cb * LANES:(cb + 1) * LANES]

    if chain:
        @pl.when(pl.program_id(0) % chunks_per_seq == 0)
        def _():
            cre_scr[...] = jnp.zeros_like(cre_scr)
            cim_scr[...] = jnp.zeros_like(cim_scr)

    row_id = lax.broadcasted_iota(jnp.int32, (SSM_SEG, sw), 0)

    def colblock(cb, carry):
        ub = u_scr[cb]
        bu_scr[...] = _dot(ub, b_ref[cb])
        are = are_ref[cb]
        aim = aim_ref[cb]
        if chain:
            hre = jnp.zeros((SSM_SEG, sw), F32)
            him = jnp.zeros((SSM_SEG, sw), F32)
        else:
            hre = h0re_ref[cb]
            him = h0im_ref[cb]
        for k in range(seg_len):
            r0 = k * SSM_SEG
            nre = are * hre - aim * him + bu_scr[r0:r0 + SSM_SEG, 0:sw]
            nim = are * him + aim * hre + bu_scr[r0:r0 + SSM_SEG, sw:cw]
            hre, him = nre, nim
            bu_scr[r0:r0 + SSM_SEG, 0:sw] = hre
            bu_scr[r0:r0 + SSM_SEG, sw:cw] = him

        if chain:
            pre, pim = _cpow(are[0:1], aim[0:1], seg_len)
            cur_re = cre_scr[cb][SSM_SEG - 1:SSM_SEG]
            cur_im = cim_scr[cb][SSM_SEG - 1:SSM_SEG]
            prev_re = jnp.zeros((SSM_SEG, sw), F32)
            prev_im = jnp.zeros((SSM_SEG, sw), F32)
            end_re = jnp.zeros((SSM_SEG, sw), F32)
            end_im = jnp.zeros((SSM_SEG, sw), F32)
            for j in range(SSM_SEG):
                prev_re = jnp.where(row_id == j, cur_re, prev_re)
                prev_im = jnp.where(row_id == j, cur_im, prev_im)
                mre, mim = _cmul(pre, pim, cur_re, cur_im)
                cur_re = mre + hre[j:j + 1]
                cur_im = mim + him[j:j + 1]
                end_re = jnp.where(row_id == j, cur_re, end_re)
                end_im = jnp.where(row_id == j, cur_im, end_im)
            cre_scr[cb] = end_re
            cim_scr[cb] = end_im
            hre_ref[cb] = end_re
            him_ref[cb] = end_im
            qre, qim = _cmul(are, aim, prev_re, prev_im)
            for k in range(seg_len):
                r0 = k * SSM_SEG
                bu_scr[r0:r0 + SSM_SEG, 0:sw] = bu_scr[r0:r0 + SSM_SEG, 0:sw] + qre
                bu_scr[r0:r0 + SSM_SEG, sw:cw] = bu_scr[r0:r0 + SSM_SEG, sw:cw] + qim
                if k + 1 < seg_len:
                    qre, qim = _cmul(are, aim, qre, qim)
        else:
            hre_ref[cb] = hre
            him_ref[cb] = him

        y = _dot(bu_scr[...].astype(BF16), c_ref[cb])
        y = y + d_ref[cb] * ub.astype(F32)
        gyp_scr[cb] = jax.nn.gelu(y, approximate=True).astype(BF16)
        return carry

    lax.fori_loop(0, n_cb, colblock, 0)

    perm_t = pt_ref[...]
    for cb in range(n_cb):
        gy_ref[:, cb * LANES:(cb + 1) * LANES] = _dot(perm_t, gyp_scr[cb]).astype(BF16)


def _s5_call(x, g, ssm, h0, *, seg_len, chunks_per_seq):
    m_rows, d_model = x.shape
    chain = h0 is None
    t_rows = SSM_SEG * seg_len
    n_chunks = m_rows // t_rows
    n_cb, _, cw = ssm['b'].shape
    sw = cw // 2
    full3 = lambda i: (0, 0, 0)
    full2 = lambda i: (0, 0)
    in_specs = [
        pl.BlockSpec((t_rows, d_model), lambda i: (i, 0)),
        pl.BlockSpec((1, d_model), full2),
        pl.BlockSpec((t_rows, t_rows), full2),
        pl.BlockSpec((t_rows, t_rows), full2),
        pl.BlockSpec((n_cb, LANES, cw), full3),
        pl.BlockSpec((n_cb, cw, LANES), full3),
        pl.BlockSpec((n_cb, SSM_SEG, sw), full3),
        pl.BlockSpec((n_cb, SSM_SEG, sw), full3),
        pl.BlockSpec((n_cb, 1, LANES), full3),
    ]
    args = [x, g, ssm['perm'], ssm['perm_t'], ssm['b'], ssm['c'], ssm['a_re'], ssm['a_im'], ssm['d']]
    if not chain:
        in_specs += [pl.BlockSpec((n_cb, SSM_SEG, sw), lambda i: (0, i, 0))] * 2
        args += [h0[0], h0[1]]
    scratch = [
        pltpu.VMEM((n_cb, t_rows, LANES), BF16),
        pltpu.VMEM((t_rows, cw), F32),
        pltpu.VMEM((n_cb, t_rows, LANES), BF16),
    ]
    if chain:
        scratch += [pltpu.VMEM((n_cb, SSM_SEG, sw), F32)] * 2
    state_shape = jax.ShapeDtypeStruct((n_cb, n_chunks * SSM_SEG, sw), F32)
    return pl.pallas_call(
        functools.partial(_s5_kernel, chain=chain, seg_len=seg_len, chunks_per_seq=chunks_per_seq,
                          n_cb=n_cb, cw=cw),
        grid=(n_chunks,),
        in_specs=in_specs,
        out_specs=[
            pl.BlockSpec((t_rows, d_model), lambda i: (i, 0)),
            pl.BlockSpec((n_cb, SSM_SEG, sw), lambda i: (0, i, 0)),
            pl.BlockSpec((n_cb, SSM_SEG, sw), lambda i: (0, i, 0)),
        ],
        out_shape=[jax.ShapeDtypeStruct((m_rows, d_model), BF16), state_shape, state_shape],
        scratch_shapes=scratch,
        compiler_params=_params("arbitrary"),
        name="s5_chain" if chain else "s5_step",
    )(*args)


def _ssm_tables(lam_re, lam_im, log_step, b_re, b_im, c_re, c_im, d, seg_len):
    n_groups, n_state, n_chan = b_re.shape
    gpb = LANES // n_chan
    n_cb = n_groups // gpb
    sw = gpb * n_state
    lam_re = lam_re.astype(F32)
    lam_im = lam_im.astype(F32)
    dt = jnp.exp(log_step.astype(F32))[:, None]
    mag = jnp.exp(lam_re * dt)
    a_re = mag * jnp.cos(lam_im * dt)
    a_im = mag * jnp.sin(lam_im * dt)
    den = lam_re * lam_re + lam_im * lam_im
    q_re = ((a_re - 1.0) * lam_re + a_im * lam_im) / den
    q_im = (a_im * lam_re - (a_re - 1.0) * lam_im) / den
    bb_re = q_re[..., None] * b_re.astype(F32) - q_im[..., None] * b_im.astype(F32)
    bb_im = q_re[..., None] * b_im.astype(F32) + q_im[..., None] * b_re.astype(F32)
    eye = jnp.eye(gpb, dtype=F32)

    def bmat(bb):
        t = bb.reshape(n_cb, gpb, n_state, n_chan)
        t = jnp.einsum('cgnp,gh->cgphn', t, eye)
        return t.reshape(n_cb, gpb * n_chan, sw)

    def cmat(cc):
        t = cc.astype(F32).reshape(n_cb, gpb, n_chan, n_state)
        t = jnp.einsum('cgpn,gh->cgnhp', t, eye)
        return t.reshape(n_cb, sw, gpb * n_chan)

    b_full = jnp.concatenate([bmat(bb_re), bmat(bb_im)], axis=-1).astype(BF16)
    c_full = jnp.concatenate([cmat(c_re), -cmat(c_im)], axis=1).astype(BF16)
    rep = lambda a: jnp.broadcast_to(a.reshape(n_cb, 1, sw), (n_cb, SSM_SEG, sw))
    t_rows = SSM_SEG * seg_len
    r = jnp.arange(t_rows)
    src = (r % SSM_SEG) * seg_len + r // SSM_SEG
    perm = (src[:, None] == jnp.arange(t_rows)[None, :]).astype(BF16)
    return dict(b=b_full, c=c_full, a_re=rep(a_re), a_im=rep(a_im),
                d=d.astype(F32).reshape(n_cb, 1, LANES), perm=perm, perm_t=perm.T)


def _glu_kernel(a_ref, wv_ref, wg_ref, x_ref, o_ref):
    a = a_ref[...]
    zv = _dot(a, wv_ref[...])
    zg = _dot(a, wg_ref[...])
    o_ref[...] = x_ref[...] + zv * jax.nn.sigmoid(zg)


def _glu_call(a, w, x, *, tm, tn):
    m_rows, d_in = a.shape
    d_out = x.shape[1]
    nj = d_out // tn
    return pl.pallas_call(
        _glu_kernel,
        grid=(m_rows // tm, nj),
        in_specs=[
            pl.BlockSpec((tm, d_in), lambda i, j: (i, 0)),
            pl.BlockSpec((d_in, tn), lambda i, j: (0, j)),
            pl.BlockSpec((d_in, tn), lambda i, j: (0, j + nj)),
            pl.BlockSpec((tm, tn), lambda i, j: (i, j)),
        ],
        out_specs=pl.BlockSpec((tm, tn), lambda i, j: (i, j)),
        out_shape=jax.ShapeDtypeStruct((m_rows, d_out), F32),
        compiler_params=_params("parallel", "arbitrary"),
        name="glu",
    )(a, w, w, x)


def _ffn_kernel(x_ref, g_ref, wa_ref, wb_ref, wo_ref, o_ref, xn_scr):
    @pl.when(pl.program_id(1) == 0)
    def _():
        x = x_ref[...]
        xn_scr[...] = _rms(x, g_ref[...]).astype(BF16)
        o_ref[...] = x

    xn = xn_scr[...]
    a = _dot(xn, wa_ref[...])
    b = _dot(xn, wb_ref[...])
    h = (a * jax.nn.sigmoid(a) * b).astype(BF16)
    o_ref[...] += _dot(h, wo_ref[...])


def _ffn_call(x, g, w_in, w_out, *, tm, tf):
    m_rows, d_model = x.shape
    d_ff = w_out.shape[0]
    nj = d_ff // tf
    return pl.pallas_call(
        _ffn_kernel,
        grid=(m_rows // tm, nj),
        in_specs=[
            pl.BlockSpec((tm, d_model), lambda i, j: (i, 0)),
            pl.BlockSpec((1, d_model), lambda i, j: (0, 0)),
            pl.BlockSpec((d_model, tf), lambda i, j: (0, j)),
            pl.BlockSpec((d_model, tf), lambda i, j: (0, j + nj)),
            pl.BlockSpec((tf, d_model), lambda i, j: (j, 0)),
        ],
        out_specs=pl.BlockSpec((tm, d_model), lambda i, j: (i, 0)),
        out_shape=jax.ShapeDtypeStruct((m_rows, d_model), F32),
        scratch_shapes=[pltpu.VMEM((tm, d_model), BF16)],
        compiler_params=_params("parallel", "arbitrary"),
        name="ffn",
    )(x, g, w_in, w_in, w_out)


def _proj_kernel(*refs, head_norm, scale, outs, head_dim, tn):
    n_out = len(outs)
    if head_norm:
        x_ref, g_ref, w_ref, hn_ref, cos_ref, sin_ref = refs[:6]
        out_refs = refs[6:6 + n_out]
    else:
        x_ref, g_ref, w_ref = refs[:3]
        out_refs = refs[3:3 + n_out]
    xn = _rms(x_ref[...], g_ref[...]).astype(BF16)
    tm = xn.shape[0]
    n_tiles = w_ref.shape[1] // tn
    if head_norm:
        half = head_dim // 2
        hd_shift = head_dim.bit_length() - 1
        gi = lax.broadcasted_iota(jnp.int32, (LANES, LANES), 0) >> hd_shift
        gj = lax.broadcasted_iota(jnp.int32, (LANES, LANES), 1) >> hd_shift
        gmat = (gi == gj).astype(BF16)
        lane = lax.broadcasted_iota(jnp.int32, (tm, LANES), 1)
        first_half = (lane & (head_dim - 1)) < half
        cos = cos_ref[...]
        sin = sin_ref[...]
        hn = hn_ref[...]
    need_t = any(kind.startswith("cols") for kind in outs)

    def matmul(t):
        return _dot(xn, w_ref[:, t * tn:(t + 1) * tn])

    def finish(t, y):
        for s in range(tn // LANES):
            cs = slice(t * tn + s * LANES, t * tn + (s + 1) * LANES)
            r = y[:, s * LANES:(s + 1) * LANES]
            if head_norm:
                ss = _dot((r * r).astype(BF16), gmat)
                yn = r * lax.rsqrt(ss * (1.0 / head_dim) + EPS) * hn
                partner = jnp.where(first_half, pltpu.roll(yn, LANES - half, 1), pltpu.roll(yn, half, 1))
                r = yn * cos + partner * sin
                if scale != 1.0:
                    r = r * scale
            rt = r.T if need_t else None
            for kind, o_ref in zip(outs, out_refs):
                if kind == "cols_f32":
                    o_ref[cs, :] = rt
                elif kind == "cols_bf16":
                    o_ref[0, 0, cs, :] = rt.astype(BF16)
                else:
                    o_ref[:, cs] = r.astype(o_ref.dtype)

    pending = matmul(0)
    for t in range(1, n_tiles):
        nxt = matmul(t)
        finish(t - 1, pending)
        pending = nxt
    finish(n_tiles - 1, pending)


def _proj_call(x, g, w, col0, n_cols, outs, *, tm, tn, rope=None, scale=1.0, head_dim=64, name="proj", seq=None):
    m_rows, d_model = x.shape
    head_norm = rope is not None
    out_specs, out_shape = [], []
    for kind in outs:
        if kind == "cols_f32":
            tiles = seq // tm
            out_specs.append(pl.BlockSpec((n_cols, tm), lambda i: (i // tiles, i % tiles)))
            out_shape.append(jax.ShapeDtypeStruct((m_rows // seq * n_cols, seq), F32))
        elif kind == "cols_bf16":
            tiles = seq // tm
            out_specs.append(pl.BlockSpec((1, 1, n_cols, tm), lambda i: (i // tiles, i % tiles, 0, 0)))
            out_shape.append(jax.ShapeDtypeStruct((m_rows // seq, tiles, n_cols, tm), BF16))
        else:
            out_specs.append(pl.BlockSpec((tm, n_cols), lambda i: (i, 0)))
            out_shape.append(jax.ShapeDtypeStruct((m_rows, n_cols), F32 if kind == "rows_f32" else BF16))
    in_specs = [
        pl.BlockSpec((tm, d_model), lambda i: (i, 0)),
        pl.BlockSpec((1, d_model), lambda i: (0, 0)),
        pl.BlockSpec((d_model, n_cols), lambda i: (0, col0 // n_cols)),
    ]
    args = [x, g, w]
    if head_norm:
        hn, cos, sin = rope
        nrep = cos.shape[0] // tm
        in_specs += [
            pl.BlockSpec((1, LANES), lambda i: (0, 0)),
            pl.BlockSpec((tm, LANES), lambda i: (i % nrep, 0)),
            pl.BlockSpec((tm, LANES), lambda i: (i % nrep, 0)),
        ]
        args += [hn, cos, sin]
    return pl.pallas_call(
        functools.partial(_proj_kernel, head_norm=head_norm, scale=scale, outs=tuple(outs), head_dim=head_dim,
                          tn=tn),
        grid=(m_rows // tm,),
        in_specs=in_specs,
        out_specs=out_specs,
        out_shape=out_shape,
        compiler_params=_params("parallel"),
        name=name,
    )(*args)


def _resmm_kernel(a_ref, w_ref, x_ref, o_ref):
    o_ref[...] = x_ref[...] + _dot(a_ref[...], w_ref[...])


def _resmm_call(a, w, x, *, tm, tn):
    m_rows, d_in = a.shape
    d_out = w.shape[1]
    return pl.pallas_call(
        _resmm_kernel,
        grid=(m_rows // tm, d_out // tn),
        in_specs=[
            pl.BlockSpec((tm, d_in), lambda i, j: (i, 0)),
            pl.BlockSpec((d_in, tn), lambda i, j: (0, j)),
            pl.BlockSpec((tm, tn), lambda i, j: (i, j)),
        ],
        out_specs=pl.BlockSpec((tm, tn), lambda i, j: (i, j)),
        out_shape=jax.ShapeDtypeStruct((m_rows, d_out), F32),
        compiler_params=_params("parallel", "arbitrary"),
        name="out_proj",
    )(a, w, x)


def _stack_maps(q, head_dim):
    lane = lax.broadcasted_iota(jnp.int32, q.shape, 1)
    zero = jnp.zeros_like(q)
    return jnp.concatenate([jnp.where(lane < head_dim, q, zero), jnp.where(lane >= head_dim, q, zero)], axis=0)


ONES_ROWS = 16


def _attn_prompt_kernel(lam_ref, sub_ref, qt_ref, k_ref, vt_ref, o_ref, acc_scr, st_scr, m_scr,
                        *, tq, head_dim, out_scale):
    qi = pl.program_id(2)
    hw = 2 * head_dim
    qt = qt_ref[0, 0]
    feat = lax.broadcasted_iota(jnp.int32, qt.shape, 0)
    zero = jnp.zeros_like(qt)
    qqt = jnp.concatenate([jnp.where(feat < head_dim, qt, zero), jnp.where(feat >= head_dim, qt, zero)], axis=1)
    acc_scr[...] = jnp.zeros_like(acc_scr)
    m_scr[...] = jnp.full_like(m_scr, NEG)
    ones = jnp.ones((ONES_ROWS, tq), BF16)

    def scores(kj, slot):
        start = pl.multiple_of(kj * tq, tq)
        st_scr[slot] = _dot(k_ref[pl.ds(start, tq), :], qqt)

    def softmax_values(kj, slot, masked):
        st = st_scr[slot]
        if masked:
            kpos = lax.broadcasted_iota(jnp.int32, st.shape, 0)
            qpos = lax.broadcasted_iota(jnp.int32, st.shape, 1)
            qpos = jnp.where(qpos >= tq, qpos - tq, qpos)
            st = jnp.where((kpos >> CHUNK_SHIFT) <= (qpos >> CHUNK_SHIFT), st, NEG)
        m_prev = m_scr[...]
        m_new = jnp.maximum(m_prev, jnp.max(st, axis=0, keepdims=True))
        m_scr[...] = m_new
        alpha = jnp.exp2(m_prev - m_new)
        pt = jnp.exp2(st - m_new).astype(BF16)
        vt1 = jnp.concatenate([vt_ref[0, kj], ones], axis=0)
        acc_scr[...] = alpha * acc_scr[...] + _dot(vt1, pt)

    scores(0, 0)

    def two_tiles(t, carry):
        scores(2 * t + 1, 1)
        softmax_values(2 * t, 0, False)
        scores(2 * t + 2, 0)
        softmax_values(2 * t + 1, 1, False)
        return carry

    lax.fori_loop(0, qi // 2, two_tiles, 0)

    @pl.when(qi % 2 == 0)
    def _():
        softmax_values(qi, 0, True)

    @pl.when(qi % 2 == 1)
    def _():
        scores(qi, 1)
        softmax_values(qi - 1, 0, False)
        softmax_values(qi, 1, True)

    acc = acc_scr[...]
    ot = acc[:hw] * (1.0 / acc[hw:hw + 1])
    ot = ot[:, :tq] - lam_ref[...] * ot[:, tq:]
    ot = ot * lax.rsqrt(jnp.mean(ot * ot, axis=0, keepdims=True) + EPS) * (sub_ref[...] * out_scale)
    o_ref[...] = ot.T.astype(o_ref.dtype)


def _attn_prompt_call(lam, sub, qt, k, vt, *, n_batch, seq, n_heads, head_dim, out_scale):
    tq = qt.shape[3]
    nq = seq // tq
    hw = 2 * head_dim
    return pl.pallas_call(
        functools.partial(_attn_prompt_kernel, tq=tq, head_dim=head_dim, out_scale=out_scale),
        grid=(n_batch, n_heads, nq),
        in_specs=[
            pl.BlockSpec((1, tq), lambda b, h, i: (0, 0)),
            pl.BlockSpec((hw, 1), lambda b, h, i: (0, 0)),
            pl.BlockSpec((1, 1, hw, tq), lambda b, h, i: (b, i, h, 0)),
            pl.BlockSpec((seq, hw), lambda b, h, i: (b, h)),
            pl.BlockSpec((1, nq, hw, tq), lambda b, h, i: (b, 0, h, 0)),
        ],
        out_specs=pl.BlockSpec((tq, hw), lambda b, h, i: (b * nq + i, h)),
        out_shape=jax.ShapeDtypeStruct(k.shape, BF16),
        scratch_shapes=[pltpu.VMEM((hw + ONES_ROWS, 2 * tq), F32), pltpu.VMEM((2, tq, 2 * tq), F32),
                        pltpu.VMEM((1, 2 * tq), F32)],
        compiler_params=_params("parallel", "parallel", "arbitrary"),
        name="attn_prompt",
    )(lam, sub, qt, k, vt)


def _attn_sample_kernel(lam_ref, sub_ref, q_ref, ck_ref, cv_ref, nk_ref, nv_ref, o_ref, qq_scr, s_scr, m_scr,
                        acc_scr, *, n_heads, head_dim, n_cache_tiles, past_len, out_scale):
    j = pl.program_id(1)
    hw = 2 * head_dim
    rows = q_ref.shape[0]
    hr = 2 * rows

    @pl.when(j == 0)
    def _():
        m_scr[...] = jnp.full_like(m_scr, NEG)
        acc_scr[...] = jnp.zeros_like(acc_scr)
        for h in range(n_heads):
            qq_scr[h] = _stack_maps(q_ref[:, h * hw:(h + 1) * hw], head_dim)

    def softmax_and_values(n_keys, v_of_head):
        m_prev = m_scr[...]
        s_tiles = [s_scr[:, t * LANES:(t + 1) * LANES] for t in range(max(n_keys // LANES, 1))]
        if n_keys < LANES:
            m_cur = jnp.max(s_scr[:, 0:n_keys], axis=-1, keepdims=True)
        else:
            m_cur = functools.reduce(jnp.maximum, s_tiles).max(axis=-1, keepdims=True)
        m_new = jnp.maximum(m_prev, m_cur)
        m_scr[...] = m_new
        alpha = jnp.exp2(m_prev - m_new)
        if n_keys < LANES:
            p = jnp.exp2(s_scr[:, 0:n_keys] - m_new[:, 0:n_keys]).astype(BF16)
        else:
            p = jnp.concatenate([jnp.exp2(st - m_new) for st in s_tiles], axis=1).astype(BF16)
        ones = jnp.ones((n_keys, hw), BF16)
        for h in range(n_heads):
            rs = slice(h * hr, (h + 1) * hr)
            v1 = jnp.concatenate([v_of_head(h), ones], axis=1)
            a = alpha[rs]
            acc_scr[h] = jnp.concatenate([a, a], axis=1) * acc_scr[h] + _dot(p[rs], v1)

    @pl.when(j < n_cache_tiles)
    def _():
        for h in range(n_heads):
            s_scr[h * hr:(h + 1) * hr, :] = _dot(qq_scr[h], ck_ref[h * hw:(h + 1) * hw, :].astype(BF16))
        n_keys = ck_ref.shape[1]
        softmax_and_values(n_keys, lambda h: cv_ref[pl.ds(h, n_keys, stride=n_heads), :].astype(BF16))

    @pl.when(j == n_cache_tiles)
    def _():
        n_new = nk_ref.shape[0]
        row = lax.broadcasted_iota(jnp.int32, (hr, n_new), 0)
        row = jnp.where(row >= rows, row - rows, row)
        col = lax.broadcasted_iota(jnp.int32, (hr, n_new), 1)
        visible = ((past_len + col) >> CHUNK_SHIFT) <= ((past_len + row) >> CHUNK_SHIFT)
        for h in range(n_heads):
            s = _dot_nt(qq_scr[h], nk_ref[:, h * hw:(h + 1) * hw])
            s_scr[h * hr:(h + 1) * hr, 0:n_new] = jnp.where(visible, s, NEG)
        softmax_and_values(n_new, lambda h: nv_ref[:, h * hw:(h + 1) * hw])
        lam = lam_ref[...]
        sub = sub_ref[...]
        for h in range(n_heads):
            acc = acc_scr[h]
            o = acc[:, :hw] * (1.0 / acc[:, hw:])
            o = o[:rows] - lam * o[rows:]
            o_ref[:, h * hw:(h + 1) * hw] = (_rms(o, sub) * out_scale).astype(o_ref.dtype)


def _attn_sample_call(lam, sub, q, cache_k, cache_v, new_k, new_v, *, n_batch, dec_seq, past_len, n_heads,
                      head_dim, tk, out_scale):
    d_model = q.shape[1]
    hw = 2 * head_dim
    n_tiles = past_len // tk
    last = n_tiles - 1
    cache_map = lambda b, j: (b * n_tiles + jnp.minimum(j, last), 0)
    cache_kt_map = lambda b, j: (b, jnp.minimum(j, last))
    return pl.pallas_call(
        functools.partial(_attn_sample_kernel, n_heads=n_heads, head_dim=head_dim, n_cache_tiles=n_tiles,
                          past_len=past_len, out_scale=out_scale),
        grid=(n_batch, n_tiles + 1),
        in_specs=[
            pl.BlockSpec((1, LANES), lambda b, j: (0, 0)),
            pl.BlockSpec((1, hw), lambda b, j: (0, 0)),
            pl.BlockSpec((dec_seq, d_model), lambda b, j: (b, 0)),
            pl.BlockSpec((d_model, tk), cache_kt_map),
            pl.BlockSpec((tk * n_heads, hw), cache_map),
            pl.BlockSpec((dec_seq, d_model), lambda b, j: (b, 0)),
            pl.BlockSpec((dec_seq, d_model), lambda b, j: (b, 0)),
        ],
        out_specs=pl.BlockSpec((dec_seq, d_model), lambda b, j: (b, 0)),
        out_shape=jax.ShapeDtypeStruct(q.shape, BF16),
        scratch_shapes=[pltpu.VMEM((n_heads, 2 * dec_seq, hw), BF16),
                        pltpu.VMEM((n_heads * 2 * dec_seq, tk), F32),
                        pltpu.VMEM((n_heads * 2 * dec_seq, LANES), F32),
                        pltpu.VMEM((n_heads, 2 * dec_seq, 2 * hw), F32)],
        compiler_params=_params("parallel", "arbitrary"),
        name="attn_sample",
    )(lam, sub, q, cache_k, cache_v, new_k, new_v)


def _rope_tables(pos, head_dim, reps):
    half = head_dim // 2
    inv = ROPE_THETA ** (-jnp.arange(half, dtype=F32) / half)
    ang = pos.astype(F32)[:, None] * inv[None, :]
    cos = jnp.cos(ang)
    sin = jnp.sin(ang)
    n_groups = LANES // head_dim
    cos_t = jnp.tile(cos, (reps, 2 * n_groups))
    sin_t = jnp.tile(jnp.concatenate([-sin, sin], axis=-1), (reps, n_groups))
    return cos_t, sin_t


def _row_tile(m_rows, cap):
    t = cap
    while m_rows % t:
        t //= 2
    return t


def _trunk(x, pos, n_seq, h0, cache, p, prompt):
    m_rows, d_model = x.shape
    slen = m_rows // n_seq
    n_a = p['norm_ssm'].shape[0]
    depth = n_a + p['norm_attn'].shape[0]
    head_dim = p['k_norm'].shape[0]
    hw = 2 * head_dim
    n_heads = d_model // hw
    k_cols = n_heads * hw
    tm = _row_tile(m_rows, 512)
    tm_big = _row_tile(m_rows, 1024)
    d_ff = p['w_ffn_out'][0].shape[0]
    tf = 512 if d_ff % 512 == 0 else 256
    row = lambda a: a.astype(F32).reshape(1, -1)

    if prompt:
        seg_len = 32
        rope_reps = 1
    else:
        seg_len = slen
        rope_reps = tm // slen
    cos_t, sin_t = _rope_tables(pos, head_dim, rope_reps)

    new_h = []
    k_f32 = v_f32 = k_b = v_b = None
    for layer in range(depth):
        if layer < n_a:
            i = layer
            tabs = _ssm_tables(p['ssm_lam_re'][i], p['ssm_lam_im'][i], p['ssm_log_step'][i], p['ssm_b_re'][i],
                               p['ssm_b_im'][i], p['ssm_c_re'][i], p['ssm_c_im'][i], p['ssm_d'][i], seg_len)
            n_cb, _, sw = tabs['a_re'].shape
            if prompt:
                t_rows = SSM_SEG * seg_len
                gy, h_re, h_im = _s5_call(x, row(p['norm_ssm'][i]), tabs, None, seg_len=seg_len,
                                          chunks_per_seq=slen // t_rows)
                last = (jnp.arange(n_seq) + 1) * (slen // seg_len) - 1
                h_re, h_im = h_re[:, last], h_im[:, last]
            else:
                to_cb = lambda a: a.astype(F32).reshape(n_seq, n_cb, sw).transpose(1, 0, 2)
                gy, h_re, h_im = _s5_call(x, row(p['norm_ssm'][i]), tabs, (to_cb(h0[0][i]), to_cb(h0[1][i])),
                                          seg_len=seg_len, chunks_per_seq=1)
            from_cb = lambda a: a.transpose(1, 0, 2).reshape(n_seq, -1)
            new_h.append((from_cb(h_re), from_cb(h_im)))
            x = _glu_call(gy, p['w_glu'][i], x, tm=tm_big, tn=512)
        else:
            if layer == n_a:
                g_kv = row(p['norm_kv'])
                hn_k = jnp.tile(row(p['k_norm']), (1, LANES // head_dim))
                k_f32, k_b = _proj_call(x, g_kv, p['w_kv'], 0, k_cols,
                                        ("cols_f32" if prompt else "rows_f32", "rows_bf16"), tm=tm, tn=512,
                                        rope=(hn_k, cos_t, sin_t), head_dim=head_dim, name="k_proj", seq=slen)
                if prompt:
                    k_f32 = k_f32.reshape(n_seq, n_heads, 2, head_dim, slen).transpose(0, 4, 1, 2, 3)
                else:
                    k_f32 = k_f32.reshape(n_seq, slen, n_heads, 2, head_dim)
                v_f32, v_b = _proj_call(x, g_kv, p['w_kv'], k_cols, p['w_kv'].shape[1] - k_cols,
                                        ("rows_f32", "cols_bf16" if prompt else "rows_bf16"), tm=tm, tn=512,
                                        name="v_proj", seq=slen)
            j = layer - n_a
            lambda_init = 0.8 - 0.6 * math.exp(-0.3 * layer)
            hn_q = jnp.tile(row(p['q_norm'][j]), (1, LANES // head_dim))
            (q,) = _proj_call(x, row(p['norm_attn'][j]), p['w_q'][j], 0, k_cols,
                              ("cols_bf16" if prompt else "rows_bf16",), tm=tm, tn=512,
                              rope=(hn_q, cos_t, sin_t), scale=head_dim ** -0.5 * LOG2_E, head_dim=head_dim,
                              name="q_proj", seq=slen)
            lam = (jnp.exp(jnp.sum(p['lambda_q1'][j].astype(F32) * p['lambda_k1'][j].astype(F32)))
                   - jnp.exp(jnp.sum(p['lambda_q2'][j].astype(F32) * p['lambda_k2'][j].astype(F32)))
                   + lambda_init)
            sub = row(p['subln'][j])
            if prompt:
                o = _attn_prompt_call(jnp.full((1, tm), lam, F32), sub.reshape(-1, 1), q, k_b, v_b, n_batch=n_seq,
                                      seq=slen, n_heads=n_heads, head_dim=head_dim, out_scale=1.0 - lambda_init)
            else:
                lam = jnp.full((1, LANES), lam, F32)
                cache_k, cache_v = cache
                past_len = cache_k.shape[1]
                cache_kt = cache_k.transpose(0, 2, 3, 4, 1).reshape(n_seq * d_model, past_len)
                o = _attn_sample_call(lam, sub, q, cache_kt,
                                      cache_v.reshape(n_seq * past_len * n_heads, hw), k_b, v_b, n_batch=n_seq,
                                      dec_seq=slen, past_len=past_len, n_heads=n_heads, head_dim=head_dim,
                                      tk=512, out_scale=1.0 - lambda_init)
            x = _resmm_call(o, p['w_o'][j], x, tm=tm_big, tn=512)
        x = _ffn_call(x, row(p['norm_ffn'][layer]), p['w_ffn_in'][layer], p['w_ffn_out'][layer], tm=tm, tf=tf)
    return x, new_h, k_f32, v_f32


def kernel(x_prompt, x_sample, state_ssm_re, state_ssm_im, cache_k, cache_v, norm_ssm, ssm_lam_re, ssm_lam_im, ssm_log_step, ssm_b_re, ssm_b_im, ssm_c_re, ssm_c_im, ssm_d, w_glu, norm_kv, w_kv, k_norm, norm_attn, w_q, q_norm, lambda_q1, lambda_k1, lambda_q2, lambda_k2, subln, w_o, norm_ffn, w_ffn_in, w_ffn_out):
    n_b, seq, d_model = x_prompt.shape
    n_db, dec_seq, _ = x_sample.shape
    past_len = cache_k.shape[1]
    head_dim = k_norm.shape[0]
    n_heads = d_model // (2 * head_dim)
    n_groups, n_state = ssm_lam_re.shape[1:]
    assert ssm_b_re.shape[3] * n_groups == d_model and LANES % ssm_b_re.shape[3] == 0
    assert 2 * head_dim == LANES and seq % (SSM_SEG * 32) == 0 and n_db % SSM_SEG == 0
    assert past_len % CHUNK == 0 and dec_seq % 16 == 0

    p = dict(norm_ssm=norm_ssm, ssm_lam_re=ssm_lam_re, ssm_lam_im=ssm_lam_im, ssm_log_step=ssm_log_step,
             ssm_b_re=ssm_b_re, ssm_b_im=ssm_b_im, ssm_c_re=ssm_c_re, ssm_c_im=ssm_c_im, ssm_d=ssm_d,
             w_glu=w_glu.astype(BF16), norm_kv=norm_kv, w_kv=w_kv.astype(BF16), k_norm=k_norm,
             norm_attn=norm_attn, w_q=w_q.astype(BF16), q_norm=q_norm, lambda_q1=lambda_q1,
             lambda_k1=lambda_k1, lambda_q2=lambda_q2, lambda_k2=lambda_k2, subln=subln,
             w_o=w_o.astype(BF16), norm_ffn=norm_ffn, w_ffn_in=w_ffn_in.astype(BF16),
             w_ffn_out=w_ffn_out.astype(BF16))

    pos_p = jnp.arange(seq, dtype=jnp.int32)
    y_p, h_p, k_p, v_p = _trunk(x_prompt.reshape(n_b * seq, d_model), pos_p, n_b, None, None, p, True)
    pos_s = past_len + jnp.arange(dec_seq, dtype=jnp.int32)
    y_s, h_s, k_s, v_s = _trunk(x_sample.reshape(n_db * dec_seq, d_model), pos_s, n_db,
                                (state_ssm_re, state_ssm_im), (cache_k, cache_v), p, False)

    def states(hs, n_seq, part):
        return jnp.stack([h[part].reshape(n_seq, n_groups, n_state) for h in hs])

    return (y_p.reshape(n_b, seq, d_model), y_s.reshape(n_db, dec_seq, d_model),
            states(h_p, n_b, 0), states(h_p, n_b, 1),
            k_p, v_p.reshape(n_b, seq, n_heads, 2 * head_dim),
            states(h_s, n_db, 0), states(h_s, n_db, 1),
            k_s, v_s.reshape(n_db, dec_seq, n_heads, 2 * head_dim))
```

```python
import functools
import math

import jax
import jax.numpy as jnp
from jax import lax
from jax.experimental import pallas as pl
from jax.experimental.pallas import tpu as pltpu

CHUNK = 64
CHUNK_SHIFT = CHUNK.bit_length() - 1
EPS = 1e-6
ROPE_THETA = 10000.0
LANES = 128
SUBLANES = 8
SSM_SEG = SUBLANES
S5_TRIP = 4
S5_AHEAD = 2
VMEM_LIMIT = 56 * 1024 * 1024

F32 = jnp.float32
BF16 = jnp.bfloat16
NEG = -1e30
LOG2_E = math.log2(math.e)


def _dot(a, b):
    return jnp.dot(a, b, preferred_element_type=F32)


def _dot_nt(a, b):
    return lax.dot_general(a, b, (((1,), (1,)), ((), ())), preferred_element_type=F32)


def _rms(x, g):
    return x * lax.rsqrt(jnp.mean(x * x, axis=-1, keepdims=True) + EPS) * g


def _params(*sem):
    return pltpu.CompilerParams(dimension_semantics=sem, vmem_limit_bytes=VMEM_LIMIT)


def _cmul(ar, ai, br, bi):
    return ar * br - ai * bi, ar * bi + ai * br


def _cpow(ar, ai, n):
    rr, ri = None, None
    br, bi = ar, ai
    while n:
        if n & 1:
            rr, ri = (br, bi) if rr is None else _cmul(rr, ri, br, bi)
        n >>= 1
        if n:
            br, bi = _cmul(br, bi, br, bi)
    return rr, ri


def _s5_kernel(*refs, chain, seg_len, chunks_per_seq, n_cb, cw):
    if chain:
        (x_ref, g_ref, p_ref, pt_ref, b_ref, c_ref, are_ref, aim_ref, d_ref,
         gy_ref, hre_ref, him_ref, u_scr, bu_scr, gyp_scr, cre_scr, cim_scr) = refs
    else:
        (x_ref, g_ref, p_ref, pt_ref, b_ref, c_ref, are_ref, aim_ref, d_ref,
         h0re_ref, h0im_ref, gy_ref, hre_ref, him_ref, u_scr, bu_scr, gyp_scr) = refs
    sw = cw // 2

    u = _rms(x_ref[...], g_ref[...]).astype(BF16)
    up = _dot(p_ref[...], u).astype(BF16)
    for cb in range(n_cb):
        u_scr[cb] = up[:, cb * LANES:(cb + 1) * LANES]

    if chain:
        @pl.when(pl.program_id(0) % chunks_per_seq == 0)
        def _():
            cre_scr[...] = jnp.zeros_like(cre_scr)
            cim_scr[...] = jnp.zeros_like(cim_scr)

    row_id = lax.broadcasted_iota(jnp.int32, (SSM_SEG, sw), 0)

    def project_in(cb, slot):
        bu_scr[slot] = _dot(u_scr[cb], b_ref[cb])

    def scan_and_project_out(cb, slot):
        h_scr = bu_scr.at[slot]
        are = are_ref[cb]
        aim = aim_ref[cb]
        if chain:
            hre = jnp.zeros((SSM_SEG, sw), F32)
            him = jnp.zeros((SSM_SEG, sw), F32)
        else:
            hre = h0re_ref[cb]
            him = h0im_ref[cb]
        for k in range(seg_len):
            r0 = k * SSM_SEG
            nre = are * hre - aim * him + h_scr[r0:r0 + SSM_SEG, 0:sw]
            nim = are * him + aim * hre + h_scr[r0:r0 + SSM_SEG, sw:cw]
            hre, him = nre, nim
            h_scr[r0:r0 + SSM_SEG, 0:sw] = hre
            h_scr[r0:r0 + SSM_SEG, sw:cw] = him

        if chain:
            pre, pim = _cpow(are[0:1], aim[0:1], seg_len)
            cur_re = cre_scr[cb][SSM_SEG - 1:SSM_SEG]
            cur_im = cim_scr[cb][SSM_SEG - 1:SSM_SEG]
            prev_re = jnp.zeros((SSM_SEG, sw), F32)
            prev_im = jnp.zeros((SSM_SEG, sw), F32)
            end_re = jnp.zeros((SSM_SEG, sw), F32)
            end_im = jnp.zeros((SSM_SEG, sw), F32)
            for j in range(SSM_SEG):
                prev_re = jnp.where(row_id == j, cur_re, prev_re)
                prev_im = jnp.where(row_id == j, cur_im, prev_im)
                mre, mim = _cmul(pre, pim, cur_re, cur_im)
                cur_re = mre + hre[j:j + 1]
                cur_im = mim + him[j:j + 1]
                end_re = jnp.where(row_id == j, cur_re, end_re)
                end_im = jnp.where(row_id == j, cur_im, end_im)
            cre_scr[cb] = end_re
            cim_scr[cb] = end_im
            hre_ref[cb] = end_re
            him_ref[cb] = end_im
            qre, qim = _cmul(are, aim, prev_re, prev_im)
            for k in range(seg_len):
                r0 = k * SSM_SEG
                h_scr[r0:r0 + SSM_SEG, 0:sw] = h_scr[r0:r0 + SSM_SEG, 0:sw] + qre
                h_scr[r0:r0 + SSM_SEG, sw:cw] = h_scr[r0:r0 + SSM_SEG, sw:cw] + qim
                if k + 1 < seg_len:
                    qre, qim = _cmul(are, aim, qre, qim)
        else:
            hre_ref[cb] = hre
            him_ref[cb] = him

        y = _dot(h_scr[...].astype(BF16), c_ref[cb])
        y = y + d_ref[cb] * u_scr[cb].astype(F32)
        gyp_scr[cb] = jax.nn.gelu(y, approximate=True).astype(BF16)

    def trip(t, carry):
        cb0 = t * S5_TRIP
        for c in range(min(S5_AHEAD, S5_TRIP)):
            project_in(cb0 + c, c)
        for c in range(S5_TRIP):
            if c + S5_AHEAD < S5_TRIP:
                project_in(cb0 + c + S5_AHEAD, c + S5_AHEAD)
            scan_and_project_out(cb0 + c, c)
        return carry

    lax.fori_loop(0, n_cb // S5_TRIP, trip, 0)

    perm_t = pt_ref[...]
    for cb in range(n_cb):
        gy_ref[:, cb * LANES:(cb + 1) * LANES] = _dot(perm_t, gyp_scr[cb]).astype(BF16)


def _s5_call(x, g, ssm, h0, *, seg_len, chunks_per_seq):
    m_rows, d_model = x.shape
    chain = h0 is None
    t_rows = SSM_SEG * seg_len
    n_chunks = m_rows // t_rows
    n_cb, _, cw = ssm['b'].shape
    sw = cw // 2
    full3 = lambda i: (0, 0, 0)
    full2 = lambda i: (0, 0)
    in_specs = [
        pl.BlockSpec((t_rows, d_model), lambda i: (i, 0)),
        pl.BlockSpec((1, d_model), full2),
        pl.BlockSpec((t_rows, t_rows), full2),
        pl.BlockSpec((t_rows, t_rows), full2),
        pl.BlockSpec((n_cb, LANES, cw), full3),
        pl.BlockSpec((n_cb, cw, LANES), full3),
        pl.BlockSpec((n_cb, SSM_SEG, sw), full3),
        pl.BlockSpec((n_cb, SSM_SEG, sw), full3),
        pl.BlockSpec((n_cb, 1, LANES), full3),
    ]
    args = [x, g, ssm['perm'], ssm['perm_t'], ssm['b'], ssm['c'], ssm['a_re'], ssm['a_im'], ssm['d']]
    if not chain:
        in_specs += [pl.BlockSpec((n_cb, SSM_SEG, sw), lambda i: (0, i, 0))] * 2
        args += [h0[0], h0[1]]
    scratch = [
        pltpu.VMEM((n_cb, t_rows, LANES), BF16),
        pltpu.VMEM((S5_TRIP, t_rows, cw), F32),
        pltpu.VMEM((n_cb, t_rows, LANES), BF16),
    ]
    if chain:
        scratch += [pltpu.VMEM((n_cb, SSM_SEG, sw), F32)] * 2
    state_shape = jax.ShapeDtypeStruct((n_cb, n_chunks * SSM_SEG, sw), F32)
    return pl.pallas_call(
        functools.partial(_s5_kernel, chain=chain, seg_len=seg_len, chunks_per_seq=chunks_per_seq,
                          n_cb=n_cb, cw=cw),
        grid=(n_chunks,),
        in_specs=in_specs,
        out_specs=[
            pl.BlockSpec((t_rows, d_model), lambda i: (i, 0)),
            pl.BlockSpec((n_cb, SSM_SEG, sw), lambda i: (0, i, 0)),
            pl.BlockSpec((n_cb, SSM_SEG, sw), lambda i: (0, i, 0)),
        ],
        out_shape=[jax.ShapeDtypeStruct((m_rows, d_model), BF16), state_shape, state_shape],
        scratch_shapes=scratch,
        compiler_params=_params("arbitrary"),
        name="s5_chain" if chain else "s5_step",
    )(*args)


def _ssm_tables(lam_re, lam_im, log_step, b_re, b_im, c_re, c_im, d, seg_len):
    n_groups, n_state, n_chan = b_re.shape
    gpb = LANES // n_chan
    n_cb = n_groups // gpb
    sw = gpb * n_state
    lam_re = lam_re.astype(F32)
    lam_im = lam_im.astype(F32)
    dt = jnp.exp(log_step.astype(F32))[:, None]
    mag = jnp.exp(lam_re * dt)
    a_re = mag * jnp.cos(lam_im * dt)
    a_im = mag * jnp.sin(lam_im * dt)
    den = lam_re * lam_re + lam_im * lam_im
    q_re = ((a_re - 1.0) * lam_re + a_im * lam_im) / den
    q_im = (a_im * lam_re - (a_re - 1.0) * lam_im) / den
    bb_re = q_re[..., None] * b_re.astype(F32) - q_im[..., None] * b_im.astype(F32)
    bb_im = q_re[..., None] * b_im.astype(F32) + q_im[..., None] * b_re.astype(F32)
    eye = jnp.eye(gpb, dtype=F32)

    def bmat(bb):
        t = bb.reshape(n_cb, gpb, n_state, n_chan)
        t = jnp.einsum('cgnp,gh->cgphn', t, eye)
        return t.reshape(n_cb, gpb * n_chan, sw)

    def cmat(cc):
        t = cc.astype(F32).reshape(n_cb, gpb, n_chan, n_state)
        t = jnp.einsum('cgpn,gh->cgnhp', t, eye)
        return t.reshape(n_cb, sw, gpb * n_chan)

    b_full = jnp.concatenate([bmat(bb_re), bmat(bb_im)], axis=-1).astype(BF16)
    c_full = jnp.concatenate([cmat(c_re), -cmat(c_im)], axis=1).astype(BF16)
    rep = lambda a: jnp.broadcast_to(a.reshape(n_cb, 1, sw), (n_cb, SSM_SEG, sw))
    t_rows = SSM_SEG * seg_len
    r = jnp.arange(t_rows)
    src = (r % SSM_SEG) * seg_len + r // SSM_SEG
    perm = (src[:, None] == jnp.arange(t_rows)[None, :]).astype(BF16)
    return dict(b=b_full, c=c_full, a_re=rep(a_re), a_im=rep(a_im),
                d=d.astype(F32).reshape(n_cb, 1, LANES), perm=perm, perm_t=perm.T)


def _glu_kernel(a_ref, wv_ref, wg_ref, x_ref, o_ref):
    a = a_ref[...]
    zv = _dot(a, wv_ref[...])
    zg = _dot(a, wg_ref[...])
    o_ref[...] = x_ref[...] + zv * jax.nn.sigmoid(zg)


def _glu_call(a, w, x, *, tm, tn):
    m_rows, d_in = a.shape
    d_out = x.shape[1]
    nj = d_out // tn
    return pl.pallas_call(
        _glu_kernel,
        grid=(m_rows // tm, nj),
        in_specs=[
            pl.BlockSpec((tm, d_in), lambda i, j: (i, 0)),
            pl.BlockSpec((d_in, tn), lambda i, j: (0, j)),
            pl.BlockSpec((d_in, tn), lambda i, j: (0, j + nj)),
            pl.BlockSpec((tm, tn), lambda i, j: (i, j)),
        ],
        out_specs=pl.BlockSpec((tm, tn), lambda i, j: (i, j)),
        out_shape=jax.ShapeDtypeStruct((m_rows, d_out), F32),
        compiler_params=_params("parallel", "arbitrary"),
        name="glu",
    )(a, w, w, x)


def _ffn_kernel(x_ref, g_ref, wa_ref, wb_ref, wo_ref, o_ref, xn_scr):
    @pl.when(pl.program_id(1) == 0)
    def _():
        x = x_ref[...]
        xn_scr[...] = _rms(x, g_ref[...]).astype(BF16)
        o_ref[...] = x

    xn = xn_scr[...]
    a = _dot(xn, wa_ref[...])
    b = _dot(xn, wb_ref[...])
    h = (a * jax.nn.sigmoid(a) * b).astype(BF16)
    o_ref[...] += _dot(h, wo_ref[...])


def _ffn_call(x, g, w_in, w_out, *, tm, tf):
    m_rows, d_model = x.shape
    d_ff = w_out.shape[0]
    nj = d_ff // tf
    return pl.pallas_call(
        _ffn_kernel,
        grid=(m_rows // tm, nj),
        in_specs=[
            pl.BlockSpec((tm, d_model), lambda i, j: (i, 0)),
            pl.BlockSpec((1, d_model), lambda i, j: (0, 0)),
            pl.BlockSpec((d_model, tf), lambda i, j: (0, j)),
            pl.BlockSpec((d_model, tf), lambda i, j: (0, j + nj)),
            pl.BlockSpec((tf, d_model), lambda i, j: (j, 0)),
        ],
        out_specs=pl.BlockSpec((tm, d_model), lambda i, j: (i, 0)),
        out_shape=jax.ShapeDtypeStruct((m_rows, d_model), F32),
        scratch_shapes=[pltpu.VMEM((tm, d_model), BF16)],
        compiler_params=_params("parallel", "arbitrary"),
        name="ffn",
    )(x, g, w_in, w_in, w_out)


def _proj_kernel(*refs, head_norm, scale, outs, head_dim, tn):
    n_out = len(outs)
    if head_norm:
        x_ref, g_ref, w_ref, hn_ref, cos_ref, sin_ref = refs[:6]
        out_refs = refs[6:6 + n_out]
    else:
        x_ref, g_ref, w_ref = refs[:3]
        out_refs = refs[3:3 + n_out]
    xn = _rms(x_ref[...], g_ref[...]).astype(BF16)
    tm = xn.shape[0]
    n_tiles = w_ref.shape[1] // tn
    if head_norm:
        half = head_dim // 2
        hd_shift = head_dim.bit_length() - 1
        gi = lax.broadcasted_iota(jnp.int32, (LANES, LANES), 0) >> hd_shift
        gj = lax.broadcasted_iota(jnp.int32, (LANES, LANES), 1) >> hd_shift
        gmat = (gi == gj).astype(BF16)
        lane = lax.broadcasted_iota(jnp.int32, (tm, LANES), 1)
        first_half = (lane & (head_dim - 1)) < half
        cos = cos_ref[...]
        sin = sin_ref[...]
        hn = hn_ref[...]
    need_t = any(kind.startswith("cols") for kind in outs)

    def matmul(t):
        return _dot(xn, w_ref[:, t * tn:(t + 1) * tn])

    def finish(t, y):
        for s in range(tn // LANES):
            cs = slice(t * tn + s * LANES, t * tn + (s + 1) * LANES)
            r = y[:, s * LANES:(s + 1) * LANES]
            if head_norm:
                ss = _dot((r * r).astype(BF16), gmat)
                yn = r * lax.rsqrt(ss * (1.0 / head_dim) + EPS) * hn
                partner = jnp.where(first_half, pltpu.roll(yn, LANES - half, 1), pltpu.roll(yn, half, 1))
                r = yn * cos + partner * sin
                if scale != 1.0:
                    r = r * scale
            rt = r.T if need_t else None
            for kind, o_ref in zip(outs, out_refs):
                if kind == "cols_f32":
                    o_ref[cs, :] = rt
                elif kind == "cols_bf16":
                    o_ref[0, 0, cs, :] = rt.astype(BF16)
                else:
                    o_ref[:, cs] = r.astype(o_ref.dtype)

    pending = matmul(0)
    for t in range(1, n_tiles):
        nxt = matmul(t)
        finish(t - 1, pending)
        pending = nxt
    finish(n_tiles - 1, pending)


def _proj_call(x, g, w, col0, n_cols, outs, *, tm, tn, rope=None, scale=1.0, head_dim=64, name="proj", seq=None):
    m_rows, d_model = x.shape
    head_norm = rope is not None
    out_specs, out_shape = [], []
    for kind in outs:
        if kind == "cols_f32":
            tiles = seq // tm
            out_specs.append(pl.BlockSpec((n_cols, tm), lambda i: (i // tiles, i % tiles)))
            out_shape.append(jax.ShapeDtypeStruct((m_rows // seq * n_cols, seq), F32))
        elif kind == "cols_bf16":
            tiles = seq // tm
            out_specs.append(pl.BlockSpec((1, 1, n_cols, tm), lambda i: (i // tiles, i % tiles, 0, 0)))
            out_shape.append(jax.ShapeDtypeStruct((m_rows // seq, tiles, n_cols, tm), BF16))
        else:
            out_specs.append(pl.BlockSpec((tm, n_cols), lambda i: (i, 0)))
            out_shape.append(jax.ShapeDtypeStruct((m_rows, n_cols), F32 if kind == "rows_f32" else BF16))
    in_specs = [
        pl.BlockSpec((tm, d_model), lambda i: (i, 0)),
        pl.BlockSpec((1, d_model), lambda i: (0, 0)),
        pl.BlockSpec((d_model, n_cols), lambda i: (0, col0 // n_cols)),
    ]
    args = [x, g, w]
    if head_norm:
        hn, cos, sin = rope
        nrep = cos.shape[0] // tm
        in_specs += [
            pl.BlockSpec((1, LANES), lambda i: (0, 0)),
            pl.BlockSpec((tm, LANES), lambda i: (i % nrep, 0)),
            pl.BlockSpec((tm, LANES), lambda i: (i % nrep, 0)),
        ]
        args += [hn, cos, sin]
    return pl.pallas_call(
        functools.partial(_proj_kernel, head_norm=head_norm, scale=scale, outs=tuple(outs), head_dim=head_dim,
                          tn=tn),
        grid=(m_rows // tm,),
        in_specs=in_specs,
        out_specs=out_specs,
        out_shape=out_shape,
        compiler_params=_params("parallel"),
        name=name,
    )(*args)


def _resmm_kernel(a_ref, w_ref, x_ref, o_ref):
    o_ref[...] = x_ref[...] + _dot(a_ref[...], w_ref[...])


def _resmm_call(a, w, x, *, tm, tn):
    m_rows, d_in = a.shape
    d_out = w.shape[1]
    return pl.pallas_call(
        _resmm_kernel,
        grid=(m_rows // tm, d_out // tn),
        in_specs=[
            pl.BlockSpec((tm, d_in), lambda i, j: (i, 0)),
            pl.BlockSpec((d_in, tn), lambda i, j: (0, j)),
            pl.BlockSpec((tm, tn), lambda i, j: (i, j)),
        ],
        out_specs=pl.BlockSpec((tm, tn), lambda i, j: (i, j)),
        out_shape=jax.ShapeDtypeStruct((m_rows, d_out), F32),
        compiler_params=_params("parallel", "arbitrary"),
        name="out_proj",
    )(a, w, x)


def _stack_maps(q, head_dim):
    lane = lax.broadcasted_iota(jnp.int32, q.shape, 1)
    zero = jnp.zeros_like(q)
    return jnp.concatenate([jnp.where(lane < head_dim, q, zero), jnp.where(lane >= head_dim, q, zero)], axis=0)


ONES_ROWS = 16


def _attn_prompt_kernel(lam_ref, sub_ref, qt_ref, k_ref, vt_ref, o_ref, acc_scr, st_scr, m_scr,
                        *, tq, head_dim, out_scale):
    qi = pl.program_id(2)
    hw = 2 * head_dim
    qt = qt_ref[0, 0]
    feat = lax.broadcasted_iota(jnp.int32, qt.shape, 0)
    zero = jnp.zeros_like(qt)
    qqt = jnp.concatenate([jnp.where(feat < head_dim, qt, zero), jnp.where(feat >= head_dim, qt, zero)], axis=1)
    acc_scr[...] = jnp.zeros_like(acc_scr)
    m_scr[...] = jnp.full_like(m_scr, NEG)
    ones = jnp.ones((ONES_ROWS, tq), BF16)

    def scores(kj, slot):
        start = pl.multiple_of(kj * tq, tq)
        st_scr[slot] = _dot(k_ref[pl.ds(start, tq), :], qqt)

    def softmax_values(kj, slot, masked):
        st = st_scr[slot]
        if masked:
            kpos = lax.broadcasted_iota(jnp.int32, st.shape, 0)
            qpos = lax.broadcasted_iota(jnp.int32, st.shape, 1)
            qpos = jnp.where(qpos >= tq, qpos - tq, qpos)
            st = jnp.where((kpos >> CHUNK_SHIFT) <= (qpos >> CHUNK_SHIFT), st, NEG)
        m_prev = m_scr[...]
        m_new = jnp.maximum(m_prev, jnp.max(st, axis=0, keepdims=True))
        m_scr[...] = m_new
        alpha = jnp.exp2(m_prev - m_new)
        pt = jnp.exp2(st - m_new).astype(BF16)
        vt1 = jnp.concatenate([vt_ref[0, kj], ones], axis=0)
        acc_scr[...] = alpha * acc_scr[...] + _dot(vt1, pt)

    scores(0, 0)

    def two_tiles(t, carry):
        scores(2 * t + 1, 1)
        softmax_values(2 * t, 0, False)
        scores(2 * t + 2, 0)
        softmax_values(2 * t + 1, 1, False)
        return carry

    lax.fori_loop(0, qi // 2, two_tiles, 0)

    @pl.when(qi % 2 == 0)
    def _():
        softmax_values(qi, 0, True)

    @pl.when(qi % 2 == 1)
    def _():
        scores(qi, 1)
        softmax_values(qi - 1, 0, False)
        softmax_values(qi, 1, True)

    acc = acc_scr[...]
    ot = acc[:hw] * (1.0 / acc[hw:hw + 1])
    ot = ot[:, :tq] - lam_ref[...] * ot[:, tq:]
    ot = ot * lax.rsqrt(jnp.mean(ot * ot, axis=0, keepdims=True) + EPS) * (sub_ref[...] * out_scale)
    o_ref[...] = ot.T.astype(o_ref.dtype)


def _attn_prompt_call(lam, sub, qt, k, vt, *, n_batch, seq, n_heads, head_dim, out_scale):
    tq = qt.shape[3]
    nq = seq // tq
    hw = 2 * head_dim
    return pl.pallas_call(
        functools.partial(_attn_prompt_kernel, tq=tq, head_dim=head_dim, out_scale=out_scale),
        grid=(n_batch, n_heads, nq),
        in_specs=[
            pl.BlockSpec((1, tq), lambda b, h, i: (0, 0)),
            pl.BlockSpec((hw, 1), lambda b, h, i: (0, 0)),
            pl.BlockSpec((1, 1, hw, tq), lambda b, h, i: (b, i, h, 0)),
            pl.BlockSpec((seq, hw), lambda b, h, i: (b, h)),
            pl.BlockSpec((1, nq, hw, tq), lambda b, h, i: (b, 0, h, 0)),
        ],
        out_specs=pl.BlockSpec((tq, hw), lambda b, h, i: (b * nq + i, h)),
        out_shape=jax.ShapeDtypeStruct(k.shape, BF16),
        scratch_shapes=[pltpu.VMEM((hw + ONES_ROWS, 2 * tq), F32), pltpu.VMEM((2, tq, 2 * tq), F32),
                        pltpu.VMEM((1, 2 * tq), F32)],
        compiler_params=_params("parallel", "parallel", "arbitrary"),
        name="attn_prompt",
    )(lam, sub, qt, k, vt)


def _attn_sample_kernel(lam_ref, sub_ref, q_ref, ck_ref, cv_ref, nk_ref, nv_ref, o_ref, qq_scr, s_scr, m_scr,
                        acc_scr, *, n_heads, head_dim, n_cache_tiles, past_len, out_scale):
    j = pl.program_id(1)
    hw = 2 * head_dim
    rows = q_ref.shape[0]
    hr = 2 * rows

    @pl.when(j == 0)
    def _():
        m_scr[...] = jnp.full_like(m_scr, NEG)
        acc_scr[...] = jnp.zeros_like(acc_scr)
        for h in range(n_heads):
            qq_scr[h] = _stack_maps(q_ref[:, h * hw:(h + 1) * hw], head_dim)

    def softmax_and_values(n_keys, v_of_head):
        m_prev = m_scr[...]
        s_tiles = [s_scr[:, t * LANES:(t + 1) * LANES] for t in range(max(n_keys // LANES, 1))]
        if n_keys < LANES:
            m_cur = jnp.max(s_scr[:, 0:n_keys], axis=-1, keepdims=True)
        else:
            m_cur = functools.reduce(jnp.maximum, s_tiles).max(axis=-1, keepdims=True)
        m_new = jnp.maximum(m_prev, m_cur)
        m_scr[...] = m_new
        alpha = jnp.exp2(m_prev - m_new)
        if n_keys < LANES:
            p = jnp.exp2(s_scr[:, 0:n_keys] - m_new[:, 0:n_keys]).astype(BF16)
        else:
            p = jnp.concatenate([jnp.exp2(st - m_new) for st in s_tiles], axis=1).astype(BF16)
        ones = jnp.ones((n_keys, hw), BF16)
        for h in range(n_heads):
            rs = slice(h * hr, (h + 1) * hr)
            v1 = jnp.concatenate([v_of_head(h), ones], axis=1)
            a = alpha[rs]
            acc_scr[h] = jnp.concatenate([a, a], axis=1) * acc_scr[h] + _dot(p[rs], v1)

    @pl.when(j < n_cache_tiles)
    def _():
        for h in range(n_heads):
            s_scr[h * hr:(h + 1) * hr, :] = _dot(qq_scr[h], ck_ref[h * hw:(h + 1) * hw, :].astype(BF16))
        n_keys = ck_ref.shape[1]
        softmax_and_values(n_keys, lambda h: cv_ref[pl.ds(h, n_keys, stride=n_heads), :].astype(BF16))

    @pl.when(j == n_cache_tiles)
    def _():
        n_new = nk_ref.shape[0]
        row = lax.broadcasted_iota(jnp.int32, (hr, n_new), 0)
        row = jnp.where(row >= rows, row - rows, row)
        col = lax.broadcasted_iota(jnp.int32, (hr, n_new), 1)
        visible = ((past_len + col) >> CHUNK_SHIFT) <= ((past_len + row) >> CHUNK_SHIFT)
        for h in range(n_heads):
            s = _dot_nt(qq_scr[h], nk_ref[:, h * hw:(h + 1) * hw])
            s_scr[h * hr:(h + 1) * hr, 0:n_new] = jnp.where(visible, s, NEG)
        softmax_and_values(n_new, lambda h: nv_ref[:, h * hw:(h + 1) * hw])
        lam = lam_ref[...]
        sub = sub_ref[...]
        for h in range(n_heads):
            acc = acc_scr[h]
            o = acc[:, :hw] * (1.0 / acc[:, hw:])
            o = o[:rows] - lam * o[rows:]
            o_ref[:, h * hw:(h + 1) * hw] = (_rms(o, sub) * out_scale).astype(o_ref.dtype)


def _attn_sample_call(lam, sub, q, cache_k, cache_v, new_k, new_v, *, n_batch, dec_seq, past_len, n_heads,
                      head_dim, tk, out_scale):
    d_model = q.shape[1]
    hw = 2 * head_dim
    n_tiles = past_len // tk
    last = n_tiles - 1
    cache_map = lambda b, j: (b * n_tiles + jnp.minimum(j, last), 0)
    cache_kt_map = lambda b, j: (b, jnp.minimum(j, last))
    return pl.pallas_call(
        functools.partial(_attn_sample_kernel, n_heads=n_heads, head_dim=head_dim, n_cache_tiles=n_tiles,
                          past_len=past_len, out_scale=out_scale),
        grid=(n_batch, n_tiles + 1),
        in_specs=[
            pl.BlockSpec((1, LANES), lambda b, j: (0, 0)),
            pl.BlockSpec((1, hw), lambda b, j: (0, 0)),
            pl.BlockSpec((dec_seq, d_model), lambda b, j: (b, 0)),
            pl.BlockSpec((d_model, tk), cache_kt_map),
            pl.BlockSpec((tk * n_heads, hw), cache_map),
            pl.BlockSpec((dec_seq, d_model), lambda b, j: (b, 0)),
            pl.BlockSpec((dec_seq, d_model), lambda b, j: (b, 0)),
        ],
        out_specs=pl.BlockSpec((dec_seq, d_model), lambda b, j: (b, 0)),
        out_shape=jax.ShapeDtypeStruct(q.shape, BF16),
        scratch_shapes=[pltpu.VMEM((n_heads, 2 * dec_seq, hw), BF16),
                        pltpu.VMEM((n_heads * 2 * dec_seq, tk), F32),
                        pltpu.VMEM((n_heads * 2 * dec_seq, LANES), F32),
                        pltpu.VMEM((n_heads, 2 * dec_seq, 2 * hw), F32)],
        compiler_params=_params("parallel", "arbitrary"),
        name="attn_sample",
    )(lam, sub, q, cache_k, cache_v, new_k, new_v)


def _rope_tables(pos, head_dim, reps):
    half = head_dim // 2
    inv = ROPE_THETA ** (-jnp.arange(half, dtype=F32) / half)
    ang = pos.astype(F32)[:, None] * inv[None, :]
    cos = jnp.cos(ang)
    sin = jnp.sin(ang)
    n_groups = LANES // head_dim
    cos_t = jnp.tile(cos, (reps, 2 * n_groups))
    sin_t = jnp.tile(jnp.concatenate([-sin, sin], axis=-1), (reps, n_groups))
    return cos_t, sin_t


def _row_tile(m_rows, cap):
    t = cap
    while m_rows % t:
        t //= 2
    return t


def _trunk(x, pos, n_seq, h0, cache, p, prompt):
    m_rows, d_model = x.shape
    slen = m_rows // n_seq
    n_a = p['norm_ssm'].shape[0]
    depth = n_a + p['norm_attn'].shape[0]
    head_dim = p['k_norm'].shape[0]
    hw = 2 * head_dim
    n_heads = d_model // hw
    k_cols = n_heads * hw
    tm = _row_tile(m_rows, 512)
    tm_big = _row_tile(m_rows, 1024)
    d_ff = p['w_ffn_out'][0].shape[0]
    tf = 512 if d_ff % 512 == 0 else 256
    row = lambda a: a.astype(F32).reshape(1, -1)

    if prompt:
        seg_len = 32
        rope_reps = 1
    else:
        seg_len = slen
        rope_reps = tm // slen
    cos_t, sin_t = _rope_tables(pos, head_dim, rope_reps)

    new_h = []
    k_f32 = v_f32 = k_b = v_b = None
    for layer in range(depth):
        if layer < n_a:
            i = layer
            tabs = _ssm_tables(p['ssm_lam_re'][i], p['ssm_lam_im'][i], p['ssm_log_step'][i], p['ssm_b_re'][i],
                               p['ssm_b_im'][i], p['ssm_c_re'][i], p['ssm_c_im'][i], p['ssm_d'][i], seg_len)
            n_cb, _, sw = tabs['a_re'].shape
            if prompt:
                t_rows = SSM_SEG * seg_len
                gy, h_re, h_im = _s5_call(x, row(p['norm_ssm'][i]), tabs, None, seg_len=seg_len,
                                          chunks_per_seq=slen // t_rows)
                last = (jnp.arange(n_seq) + 1) * (slen // seg_len) - 1
                h_re, h_im = h_re[:, last], h_im[:, last]
            else:
                to_cb = lambda a: a.astype(F32).reshape(n_seq, n_cb, sw).transpose(1, 0, 2)
                gy, h_re, h_im = _s5_call(x, row(p['norm_ssm'][i]), tabs, (to_cb(h0[0][i]), to_cb(h0[1][i])),
                                          seg_len=seg_len, chunks_per_seq=1)
            from_cb = lambda a: a.transpose(1, 0, 2).reshape(n_seq, -1)
            new_h.append((from_cb(h_re), from_cb(h_im)))
            x = _glu_call(gy, p['w_glu'][i], x, tm=tm_big, tn=512)
        else:
            if layer == n_a:
                g_kv = row(p['norm_kv'])
                hn_k = jnp.tile(row(p['k_norm']), (1, LANES // head_dim))
                k_f32, k_b = _proj_call(x, g_kv, p['w_kv'], 0, k_cols,
                                        ("cols_f32" if prompt else "rows_f32", "rows_bf16"), tm=tm, tn=512,
                                        rope=(hn_k, cos_t, sin_t), head_dim=head_dim, name="k_proj", seq=slen)
                if prompt:
                    k_f32 = k_f32.reshape(n_seq, n_heads, 2, head_dim, slen).transpose(0, 4, 1, 2, 3)
                else:
                    k_f32 = k_f32.reshape(n_seq, slen, n_heads, 2, head_dim)
                v_f32, v_b = _proj_call(x, g_kv, p['w_kv'], k_cols, p['w_kv'].shape[1] - k_cols,
                                        ("rows_f32", "cols_bf16" if prompt else "rows_bf16"), tm=tm, tn=512,
                                        name="v_proj", seq=slen)
            j = layer - n_a
            lambda_init = 0.8 - 0.6 * math.exp(-0.3 * layer)
            hn_q = jnp.tile(row(p['q_norm'][j]), (1, LANES // head_dim))
            (q,) = _proj_call(x, row(p['norm_attn'][j]), p['w_q'][j], 0, k_cols,
                              ("cols_bf16" if prompt else "rows_bf16",), tm=tm, tn=512,
                              rope=(hn_q, cos_t, sin_t), scale=head_dim ** -0.5 * LOG2_E, head_dim=head_dim,
                              name="q_proj", seq=slen)
            lam = (jnp.exp(jnp.sum(p['lambda_q1'][j].astype(F32) * p['lambda_k1'][j].astype(F32)))
                   - jnp.exp(jnp.sum(p['lambda_q2'][j].astype(F32) * p['lambda_k2'][j].astype(F32)))
                   + lambda_init)
            sub = row(p['subln'][j])
            if prompt:
                o = _attn_prompt_call(jnp.full((1, tm), lam, F32), sub.reshape(-1, 1), q, k_b, v_b, n_batch=n_seq,
                                      seq=slen, n_heads=n_heads, head_dim=head_dim, out_scale=1.0 - lambda_init)
            else:
                lam = jnp.full((1, LANES), lam, F32)
                cache_k, cache_v = cache
                past_len = cache_k.shape[1]
                cache_kt = cache_k.transpose(0, 2, 3, 4, 1).reshape(n_seq * d_model, past_len)
                o = _attn_sample_call(lam, sub, q, cache_kt,
                                      cache_v.reshape(n_seq * past_len * n_heads, hw), k_b, v_b, n_batch=n_seq,
                                      dec_seq=slen, past_len=past_len, n_heads=n_heads, head_dim=head_dim,
                                      tk=512, out_scale=1.0 - lambda_init)
            x = _resmm_call(o, p['w_o'][j], x, tm=tm_big, tn=512)
        x = _ffn_call(x, row(p['norm_ffn'][layer]), p['w_ffn_in'][layer], p['w_ffn_out'][layer], tm=tm, tf=tf)
    return x, new_h, k_f32, v_f32


def kernel(x_prompt, x_sample, state_ssm_re, state_ssm_im, cache_k, cache_v, norm_ssm, ssm_lam_re, ssm_lam_im, ssm_log_step, ssm_b_re, ssm_b_im, ssm_c_re, ssm_c_im, ssm_d, w_glu, norm_kv, w_kv, k_norm, norm_attn, w_q, q_norm, lambda_q1, lambda_k1, lambda_q2, lambda_k2, subln, w_o, norm_ffn, w_ffn_in, w_ffn_out):
    n_b, seq, d_model = x_prompt.shape
    n_db, dec_seq, _ = x_sample.shape
    past_len = cache_k.shape[1]
    head_dim = k_norm.shape[0]
    n_heads = d_model // (2 * head_dim)
    n_groups, n_state = ssm_lam_re.shape[1:]
    assert ssm_b_re.shape[3] * n_groups == d_model and LANES % ssm_b_re.shape[3] == 0
    assert 2 * head_dim == LANES and seq % (SSM_SEG * 32) == 0 and n_db % SSM_SEG == 0
    assert past_len % CHUNK == 0 and dec_seq % 16 == 0

    p = dict(norm_ssm=norm_ssm, ssm_lam_re=ssm_lam_re, ssm_lam_im=ssm_lam_im, ssm_log_step=ssm_log_step,
             ssm_b_re=ssm_b_re, ssm_b_im=ssm_b_im, ssm_c_re=ssm_c_re, ssm_c_im=ssm_c_im, ssm_d=ssm_d,
             w_glu=w_glu.astype(BF16), norm_kv=norm_kv, w_kv=w_kv.astype(BF16), k_norm=k_norm,
             norm_attn=norm_attn, w_q=w_q.astype(BF16), q_norm=q_norm, lambda_q1=lambda_q1,
             lambda_k1=lambda_k1, lambda_q2=lambda_q2, lambda_k2=lambda_k2, subln=subln,
             w_o=w_o.astype(BF16), norm_ffn=norm_ffn, w_ffn_in=w_ffn_in.astype(BF16),
             w_ffn_out=w_ffn_out.astype(BF16))

    pos_p = jnp.arange(seq, dtype=jnp.int32)
    y_p, h_p, k_p, v_p = _trunk(x_prompt.reshape(n_b * seq, d_model), pos_p, n_b, None, None, p, True)
    pos_s = past_len + jnp.arange(dec_seq, dtype=jnp.int32)
    y_s, h_s, k_s, v_s = _trunk(x_sample.reshape(n_db * dec_seq, d_model), pos_s, n_db,
                                (state_ssm_re, state_ssm_im), (cache_k, cache_v), p, False)

    def states(hs, n_seq, part):
        return jnp.stack([h[part].reshape(n_seq, n_groups, n_state) for h in hs])

    return (y_p.reshape(n_b, seq, d_model), y_s.reshape(n_db, dec_seq, d_model),
            states(h_p, n_b, 0), states(h_p, n_b, 1),
            k_p, v_p.reshape(n_b, seq, n_heads, 2 * head_dim),
            states(h_s, n_db, 0), states(h_s, n_db, 1),
            k_s, v_s.reshape(n_db, dec_seq, n_heads, 2 * head_dim))
```

```python
import functools
import math

import jax
import jax.numpy as jnp
from jax import lax
from jax.experimental import pallas as pl
from jax.experimental.pallas import tpu as pltpu

CHUNK = 64
CHUNK_SHIFT = CHUNK.bit_length() - 1
EPS = 1e-6
ROPE_THETA = 10000.0
LANES = 128
SUBLANES = 8
SSM_SEG = SUBLANES
S5_TRIP = 4
S5_AHEAD = 2
VMEM_LIMIT = 56 * 1024 * 1024

F32 = jnp.float32
BF16 = jnp.bfloat16
NEG = -1e30
LOG2_E = math.log2(math.e)


def _dot(a, b):
    return jnp.dot(a, b, preferred_element_type=F32)


def _dot_nt(a, b):
    return lax.dot_general(a, b, (((1,), (1,)), ((), ())), preferred_element_type=F32)


def _rms(x, g):
    return x * lax.rsqrt(jnp.mean(x * x, axis=-1, keepdims=True) + EPS) * g


def _params(*sem):
    return pltpu.CompilerParams(dimension_semantics=sem, vmem_limit_bytes=VMEM_LIMIT)


def _cmul(ar, ai, br, bi):
    return ar * br - ai * bi, ar * bi + ai * br


def _cpow(ar, ai, n):
    rr, ri = None, None
    br, bi = ar, ai
    while n:
        if n & 1:
            rr, ri = (br, bi) if rr is None else _cmul(rr, ri, br, bi)
        n >>= 1
        if n:
            br, bi = _cmul(br, bi, br, bi)
    return rr, ri


def _s5_kernel(*refs, chain, seg_len, chunks_per_seq, n_cb, cw):
    if chain:
        (x_ref, g_ref, p_ref, pt_ref, b_ref, c_ref, are_ref, aim_ref, d_ref,
         gy_ref, hre_ref, him_ref, u_scr, bu_scr, gyp_scr, cre_scr, cim_scr) = refs
    else:
        (x_ref, g_ref, p_ref, pt_ref, b_ref, c_ref, are_ref, aim_ref, d_ref,
         h0re_ref, h0im_ref, gy_ref, hre_ref, him_ref, u_scr, bu_scr, gyp_scr) = refs
    sw = cw // 2

    u = _rms(x_ref[...], g_ref[...]).astype(BF16)
    up = _dot(p_ref[...], u).astype(BF16)
    for cb in range(n_cb):
        u_scr[cb] = up[:, cb * LANES:(cb + 1) * LANES]

    if chain:
        @pl.when(pl.program_id(0) % chunks_per_seq == 0)
        def _():
            cre_scr[...] = jnp.zeros_like(cre_scr)
            cim_scr[...] = jnp.zeros_like(cim_scr)

    row_id = lax.broadcasted_iota(jnp.int32, (SSM_SEG, sw), 0)

    def project_in(cb, slot):
        bu_scr[slot] = _dot(u_scr[cb], b_ref[cb])

    def scan_and_project_out(cb, slot):
        h_scr = bu_scr.at[slot]
        are = are_ref[cb]
        aim = aim_ref[cb]
        if chain:
            hre = jnp.zeros((SSM_SEG, sw), F32)
            him = jnp.zeros((SSM_SEG, sw), F32)
        else:
            hre = h0re_ref[cb]
            him = h0im_ref[cb]
        for k in range(seg_len):
            r0 = k * SSM_SEG
            nre = are * hre - aim * him + h_scr[r0:r0 + SSM_SEG, 0:sw]
            nim = are * him + aim * hre + h_scr[r0:r0 + SSM_SEG, sw:cw]
            hre, him = nre, nim
            h_scr[r0:r0 + SSM_SEG, 0:sw] = hre
            h_scr[r0:r0 + SSM_SEG, sw:cw] = him

        if chain:
            pre, pim = _cpow(are[0:1], aim[0:1], seg_len)
            cur_re = cre_scr[cb][SSM_SEG - 1:SSM_SEG]
            cur_im = cim_scr[cb][SSM_SEG - 1:SSM_SEG]
            prev_re = jnp.zeros((SSM_SEG, sw), F32)
            prev_im = jnp.zeros((SSM_SEG, sw), F32)
            end_re = jnp.zeros((SSM_SEG, sw), F32)
            end_im = jnp.zeros((SSM_SEG, sw), F32)
            for j in range(SSM_SEG):
                prev_re = jnp.where(row_id == j, cur_re, prev_re)
                prev_im = jnp.where(row_id == j, cur_im, prev_im)
                mre, mim = _cmul(pre, pim, cur_re, cur_im)
                cur_re = mre + hre[j:j + 1]
                cur_im = mim + him[j:j + 1]
                end_re = jnp.where(row_id == j, cur_re, end_re)
                end_im = jnp.where(row_id == j, cur_im, end_im)
            cre_scr[cb] = end_re
            cim_scr[cb] = end_im
            hre_ref[cb] = end_re
            him_ref[cb] = end_im
            qre, qim = _cmul(are, aim, prev_re, prev_im)
            for k in range(seg_len):
                r0 = k * SSM_SEG
                h_scr[r0:r0 + SSM_SEG, 0:sw] = h_scr[r0:r0 + SSM_SEG, 0:sw] + qre
                h_scr[r0:r0 + SSM_SEG, sw:cw] = h_scr[r0:r0 + SSM_SEG, sw:cw] + qim
                if k + 1 < seg_len:
                    qre, qim = _cmul(are, aim, qre, qim)
        else:
            hre_ref[cb] = hre
            him_ref[cb] = him

        y = _dot(h_scr[...].astype(BF16), c_ref[cb])
        y = y + d_ref[cb] * u_scr[cb].astype(F32)
        gyp_scr[cb] = jax.nn.gelu(y, approximate=True).astype(BF16)

    def trip(t, carry):
        cb0 = t * S5_TRIP
        for c in range(min(S5_AHEAD, S5_TRIP)):
            project_in(cb0 + c, c)
        for c in range(S5_TRIP):
            if c + S5_AHEAD < S5_TRIP:
                project_in(cb0 + c + S5_AHEAD, c + S5_AHEAD)
            scan_and_project_out(cb0 + c, c)
        return carry

    lax.fori_loop(0, n_cb // S5_TRIP, trip, 0)

    perm_t = pt_ref[...]
    for cb in range(n_cb):
        gy_ref[:, cb * LANES:(cb + 1) * LANES] = _dot(perm_t, gyp_scr[cb]).astype(BF16)


def _s5_call(x, g, ssm, h0, *, seg_len, chunks_per_seq):
    m_rows, d_model = x.shape
    chain = h0 is None
    t_rows = SSM_SEG * seg_len
    n_chunks = m_rows // t_rows
    n_cb, _, cw = ssm['b'].shape
    sw = cw // 2
    full3 = lambda i: (0, 0, 0)
    full2 = lambda i: (0, 0)
    in_specs = [
        pl.BlockSpec((t_rows, d_model), lambda i: (i, 0)),
        pl.BlockSpec((1, d_model), full2),
        pl.BlockSpec((t_rows, t_rows), full2),
        pl.BlockSpec((t_rows, t_rows), full2),
        pl.BlockSpec((n_cb, LANES, cw), full3),
        pl.BlockSpec((n_cb, cw, LANES), full3),
        pl.BlockSpec((n_cb, SSM_SEG, sw), full3),
        pl.BlockSpec((n_cb, SSM_SEG, sw), full3),
        pl.BlockSpec((n_cb, 1, LANES), full3),
    ]
    args = [x, g, ssm['perm'], ssm['perm_t'], ssm['b'], ssm['c'], ssm['a_re'], ssm['a_im'], ssm['d']]
    if not chain:
        in_specs += [pl.BlockSpec((n_cb, SSM_SEG, sw), lambda i: (0, i, 0))] * 2
        args += [h0[0], h0[1]]
    scratch = [
        pltpu.VMEM((n_cb, t_rows, LANES), BF16),
        pltpu.VMEM((S5_TRIP, t_rows, cw), F32),
        pltpu.VMEM((n_cb, t_rows, LANES), BF16),
    ]
    if chain:
        scratch += [pltpu.VMEM((n_cb, SSM_SEG, sw), F32)] * 2
    state_shape = jax.ShapeDtypeStruct((n_cb, n_chunks * SSM_SEG, sw), F32)
    return pl.pallas_call(
        functools.partial(_s5_kernel, chain=chain, seg_len=seg_len, chunks_per_seq=chunks_per_seq,
                          n_cb=n_cb, cw=cw),
        grid=(n_chunks,),
        in_specs=in_specs,
        out_specs=[
            pl.BlockSpec((t_rows, d_model), lambda i: (i, 0)),
            pl.BlockSpec((n_cb, SSM_SEG, sw), lambda i: (0, i, 0)),
            pl.BlockSpec((n_cb, SSM_SEG, sw), lambda i: (0, i, 0)),
        ],
        out_shape=[jax.ShapeDtypeStruct((m_rows, d_model), BF16), state_shape, state_shape],
        scratch_shapes=scratch,
        compiler_params=_params("arbitrary"),
        name="s5_chain" if chain else "s5_step",
    )(*args)


def _ssm_tables(lam_re, lam_im, log_step, b_re, b_im, c_re, c_im, d, seg_len):
    n_groups, n_state, n_chan = b_re.shape
    gpb = LANES // n_chan
    n_cb = n_groups // gpb
    sw = gpb * n_state
    lam_re = lam_re.astype(F32)
    lam_im = lam_im.astype(F32)
    dt = jnp.exp(log_step.astype(F32))[:, None]
    mag = jnp.exp(lam_re * dt)
    a_re = mag * jnp.cos(lam_im * dt)
    a_im = mag * jnp.sin(lam_im * dt)
    den = lam_re * lam_re + lam_im * lam_im
    q_re = ((a_re - 1.0) * lam_re + a_im * lam_im) / den
    q_im = (a_im * lam_re - (a_re - 1.0) * lam_im) / den
    bb_re = q_re[..., None] * b_re.astype(F32) - q_im[..., None] * b_im.astype(F32)
    bb_im = q_re[..., None] * b_im.astype(F32) + q_im[..., None] * b_re.astype(F32)
    eye = jnp.eye(gpb, dtype=F32)

    def bmat(bb):
        t = bb.reshape(n_cb, gpb, n_state, n_chan)
        t = jnp.einsum('cgnp,gh->cgphn', t, eye)
        return t.reshape(n_cb, gpb * n_chan, sw)

    def cmat(cc):
        t = cc.astype(F32).reshape(n_cb, gpb, n_chan, n_state)
        t = jnp.einsum('cgpn,gh->cgnhp', t, eye)
        return t.reshape(n_cb, sw, gpb * n_chan)

    b_full = jnp.concatenate([bmat(bb_re), bmat(bb_im)], axis=-1).astype(BF16)
    c_full = jnp.concatenate([cmat(c_re), -cmat(c_im)], axis=1).astype(BF16)
    rep = lambda a: jnp.broadcast_to(a.reshape(n_cb, 1, sw), (n_cb, SSM_SEG, sw))
    t_rows = SSM_SEG * seg_len
    r = jnp.arange(t_rows)
    src = (r % SSM_SEG) * seg_len + r // SSM_SEG
    perm = (src[:, None] == jnp.arange(t_rows)[None, :]).astype(BF16)
    return dict(b=b_full, c=c_full, a_re=rep(a_re), a_im=rep(a_im),
                d=d.astype(F32).reshape(n_cb, 1, LANES), perm=perm, perm_t=perm.T)


def _glu_kernel(a_ref, wv_ref, wg_ref, x_ref, o_ref):
    a = a_ref[...]
    zv = _dot(a, wv_ref[...])
    zg = _dot(a, wg_ref[...])
    o_ref[...] = x_ref[...] + zv * jax.nn.sigmoid(zg)


def _glu_call(a, w, x, *, tm, tn):
    m_rows, d_in = a.shape
    d_out = x.shape[1]
    nj = d_out // tn
    return pl.pallas_call(
        _glu_kernel,
        grid=(m_rows // tm, nj),
        in_specs=[
            pl.BlockSpec((tm, d_in), lambda i, j: (i, 0)),
            pl.BlockSpec((d_in, tn), lambda i, j: (0, j)),
            pl.BlockSpec((d_in, tn), lambda i, j: (0, j + nj)),
            pl.BlockSpec((tm, tn), lambda i, j: (i, j)),
        ],
        out_specs=pl.BlockSpec((tm, tn), lambda i, j: (i, j)),
        out_shape=jax.ShapeDtypeStruct((m_rows, d_out), F32),
        compiler_params=_params("parallel", "arbitrary"),
        name="glu",
    )(a, w, w, x)


def _ffn_kernel(x_ref, g_ref, wa_ref, wb_ref, wo_ref, o_ref, xn_scr):
    @pl.when(pl.program_id(1) == 0)
    def _():
        x = x_ref[...]
        xn_scr[...] = _rms(x, g_ref[...]).astype(BF16)
        o_ref[...] = x

    xn = xn_scr[...]
    a = _dot(xn, wa_ref[...])
    b = _dot(xn, wb_ref[...])
    h = (a * jax.nn.sigmoid(a) * b).astype(BF16)
    o_ref[...] += _dot(h, wo_ref[...])


def _ffn_call(x, g, w_in, w_out, *, tm, tf):
    m_rows, d_model = x.shape
    d_ff = w_out.shape[0]
    nj = d_ff // tf
    return pl.pallas_call(
        _ffn_kernel,
        grid=(m_rows // tm, nj),
        in_specs=[
            pl.BlockSpec((tm, d_model), lambda i, j: (i, 0)),
            pl.BlockSpec((1, d_model), lambda i, j: (0, 0)),
            pl.BlockSpec((d_model, tf), lambda i, j: (0, j)),
            pl.BlockSpec((d_model, tf), lambda i, j: (0, j + nj)),
            pl.BlockSpec((tf, d_model), lambda i, j: (j, 0)),
        ],
        out_specs=pl.BlockSpec((tm, d_model), lambda i, j: (i, 0)),
        out_shape=jax.ShapeDtypeStruct((m_rows, d_model), F32),
        scratch_shapes=[pltpu.VMEM((tm, d_model), BF16)],
        compiler_params=_params("parallel", "arbitrary"),
        name="ffn",
    )(x, g, w_in, w_in, w_out)


def _proj_kernel(*refs, head_norm, scale, outs, head_dim, tn):
    n_out = len(outs)
    if head_norm:
        x_ref, g_ref, w_ref, hn_ref, cos_ref, sin_ref = refs[:6]
        out_refs = refs[6:6 + n_out]
    else:
        x_ref, g_ref, w_ref = refs[:3]
        out_refs = refs[3:3 + n_out]
    xn = _rms(x_ref[...], g_ref[...]).astype(BF16)
    tm = xn.shape[0]
    n_tiles = w_ref.shape[1] // tn
    if head_norm:
        half = head_dim // 2
        hd_shift = head_dim.bit_length() - 1
        gi = lax.broadcasted_iota(jnp.int32, (LANES, LANES), 0) >> hd_shift
        gj = lax.broadcasted_iota(jnp.int32, (LANES, LANES), 1) >> hd_shift
        gmat = (gi == gj).astype(BF16)
        lane = lax.broadcasted_iota(jnp.int32, (tm, LANES), 1)
        first_half = (lane & (head_dim - 1)) < half
        cos = cos_ref[...]
        sin = sin_ref[...]
        hn = hn_ref[...]
    need_t = any(kind.startswith("cols") for kind in outs)

    def matmul(t):
        return _dot(xn, w_ref[:, t * tn:(t + 1) * tn])

    def finish(t, y):
        for s in range(tn // LANES):
            cs = slice(t * tn + s * LANES, t * tn + (s + 1) * LANES)
            r = y[:, s * LANES:(s + 1) * LANES]
            if head_norm:
                ss = _dot((r * r).astype(BF16), gmat)
                yn = r * lax.rsqrt(ss * (1.0 / head_dim) + EPS) * hn
                partner = jnp.where(first_half, pltpu.roll(yn, LANES - half, 1), pltpu.roll(yn, half, 1))
                r = yn * cos + partner * sin
                if scale != 1.0:
                    r = r * scale
            rt = r.T if need_t else None
            for kind, o_ref in zip(outs, out_refs):
                if kind == "cols_f32":
                    o_ref[cs, :] = rt
                elif kind == "cols_bf16":
                    o_ref[0, 0, cs, :] = rt.astype(BF16)
                else:
                    o_ref[:, cs] = r.astype(o_ref.dtype)

    pending = matmul(0)
    for t in range(1, n_tiles):
        nxt = matmul(t)
        finish(t - 1, pending)
        pending = nxt
    finish(n_tiles - 1, pending)


def _proj_call(x, g, w, col0, n_cols, outs, *, tm, tn, rope=None, scale=1.0, head_dim=64, name="proj", seq=None):
    m_rows, d_model = x.shape
    head_norm = rope is not None
    out_specs, out_shape = [], []
    for kind in outs:
        if kind == "cols_f32":
            tiles = seq // tm
            out_specs.append(pl.BlockSpec((n_cols, tm), lambda i: (i // tiles, i % tiles)))
            out_shape.append(jax.ShapeDtypeStruct((m_rows // seq * n_cols, seq), F32))
        elif kind == "cols_bf16":
            tiles = seq // tm
            out_specs.append(pl.BlockSpec((1, 1, n_cols, tm), lambda i: (i // tiles, i % tiles, 0, 0)))
            out_shape.append(jax.ShapeDtypeStruct((m_rows // seq, tiles, n_cols, tm), BF16))
        else:
            out_specs.append(pl.BlockSpec((tm, n_cols), lambda i: (i, 0)))
            out_shape.append(jax.ShapeDtypeStruct((m_rows, n_cols), F32 if kind == "rows_f32" else BF16))
    in_specs = [
        pl.BlockSpec((tm, d_model), lambda i: (i, 0)),
        pl.BlockSpec((1, d_model), lambda i: (0, 0)),
        pl.BlockSpec((d_model, n_cols), lambda i: (0, col0 // n_cols)),
    ]
    args = [x, g, w]
    if head_norm:
        hn, cos, sin = rope
        nrep = cos.shape[0] // tm
        in_specs += [
            pl.BlockSpec((1, LANES), lambda i: (0, 0)),
            pl.BlockSpec((tm, LANES), lambda i: (i % nrep, 0)),
            pl.BlockSpec((tm, LANES), lambda i: (i % nrep, 0)),
        ]
        args += [hn, cos, sin]
    return pl.pallas_call(
        functools.partial(_proj_kernel, head_norm=head_norm, scale=scale, outs=tuple(outs), head_dim=head_dim,
                          tn=tn),
        grid=(m_rows // tm,),
        in_specs=in_specs,
        out_specs=out_specs,
        out_shape=out_shape,
        compiler_params=_params("parallel"),
        name=name,
    )(*args)


def _resmm_kernel(a_ref, w_ref, x_ref, o_ref):
    o_ref[...] = x_ref[...] + _dot(a_ref[...], w_ref[...])


def _resmm_call(a, w, x, *, tm, tn):
    m_rows, d_in = a.shape
    d_out = w.shape[1]
    return pl.pallas_call(
        _resmm_kernel,
        grid=(m_rows // tm, d_out // tn),
        in_specs=[
            pl.BlockSpec((tm, d_in), lambda i, j: (i, 0)),
            pl.BlockSpec((d_in, tn), lambda i, j: (0, j)),
            pl.BlockSpec((tm, tn), lambda i, j: (i, j)),
        ],
        out_specs=pl.BlockSpec((tm, tn), lambda i, j: (i, j)),
        out_shape=jax.ShapeDtypeStruct((m_rows, d_out), F32),
        compiler_params=_params("parallel", "arbitrary"),
        name="out_proj",
    )(a, w, x)


def _stack_maps(q, head_dim):
    lane = lax.broadcasted_iota(jnp.int32, q.shape, 1)
    zero = jnp.zeros_like(q)
    return jnp.concatenate([jnp.where(lane < head_dim, q, zero), jnp.where(lane >= head_dim, q, zero)], axis=0)


ONES_ROWS = 16


def _attn_prompt_kernel(lam_ref, sub_ref, qt_ref, k_ref, vt_ref, o_ref, acc_scr, st_scr, m_scr, qq_scr,
                        *, tq, head_dim, out_scale):
    hw = 2 * head_dim
    nq = qt_ref.shape[1]
    ones = jnp.ones((ONES_ROWS, tq), BF16)

    def stage_queries(qi):
        qt = qt_ref[0, qi]
        feat = lax.broadcasted_iota(jnp.int32, qt.shape, 0)
        zero = jnp.zeros_like(qt)
        qq_scr[...] = jnp.concatenate(
            [jnp.where(feat < head_dim, qt, zero), jnp.where(feat >= head_dim, qt, zero)], axis=1)

    def scores(kj, slot):
        start = pl.multiple_of(kj * tq, tq)
        st_scr[slot] = _dot(k_ref[pl.ds(start, tq), :], qq_scr[...])

    def softmax_values(kj, slot, masked):
        st = st_scr[slot]
        if masked:
            kpos = lax.broadcasted_iota(jnp.int32, st.shape, 0)
            qpos = lax.broadcasted_iota(jnp.int32, st.shape, 1)
            qpos = jnp.where(qpos >= tq, qpos - tq, qpos)
            st = jnp.where((kpos >> CHUNK_SHIFT) <= (qpos >> CHUNK_SHIFT), st, NEG)
        m_prev = m_scr[...]
        m_new = jnp.maximum(m_prev, jnp.max(st, axis=0, keepdims=True))
        m_scr[...] = m_new
        alpha = jnp.exp2(m_prev - m_new)
        pt = jnp.exp2(st - m_new).astype(BF16)
        vt1 = jnp.concatenate([vt_ref[0, kj], ones], axis=0)
        acc_scr[...] = alpha * acc_scr[...] + _dot(vt1, pt)

    def two_tiles(t, carry):
        scores(2 * t + 1, 1)
        softmax_values(2 * t, 0, False)
        scores(2 * t + 2, 0)
        softmax_values(2 * t + 1, 1, False)
        return carry

    def query_tile(qi, carry):
        acc_scr[...] = jnp.zeros_like(acc_scr)
        m_scr[...] = jnp.full_like(m_scr, NEG)
        lax.fori_loop(0, qi // 2, two_tiles, 0)

        @pl.when(qi % 2 == 0)
        def _():
            softmax_values(qi, 0, True)

        @pl.when(qi % 2 == 1)
        def _():
            scores(qi, 1)
            softmax_values(qi - 1, 0, False)
            softmax_values(qi, 1, True)

        stage_queries(jnp.minimum(qi + 1, nq - 1))
        scores(0, 0)
        acc = acc_scr[...]
        ot = acc[:hw] * (1.0 / acc[hw:hw + 1])
        ot = ot[:, :tq] - lam_ref[...] * ot[:, tq:]
        ot = ot * lax.rsqrt(jnp.mean(ot * ot, axis=0, keepdims=True) + EPS) * (sub_ref[...] * out_scale)
        o_ref[pl.ds(pl.multiple_of(qi * tq, tq), tq), :] = ot.T.astype(o_ref.dtype)
        return carry

    stage_queries(0)
    scores(0, 0)
    lax.fori_loop(0, nq, query_tile, 0)


def _attn_prompt_call(lam, sub, qt, k, vt, *, n_batch, seq, n_heads, head_dim, out_scale):
    tq = qt.shape[3]
    nq = seq // tq
    hw = 2 * head_dim
    return pl.pallas_call(
        functools.partial(_attn_prompt_kernel, tq=tq, head_dim=head_dim, out_scale=out_scale),
        grid=(n_batch, n_heads),
        in_specs=[
            pl.BlockSpec((1, tq), lambda b, h: (0, 0)),
            pl.BlockSpec((hw, 1), lambda b, h: (0, 0)),
            pl.BlockSpec((1, nq, hw, tq), lambda b, h: (b, 0, h, 0)),
            pl.BlockSpec((seq, hw), lambda b, h: (b, h)),
            pl.BlockSpec((1, nq, hw, tq), lambda b, h: (b, 0, h, 0)),
        ],
        out_specs=pl.BlockSpec((seq, hw), lambda b, h: (b, h)),
        out_shape=jax.ShapeDtypeStruct(k.shape, BF16),
        scratch_shapes=[pltpu.VMEM((hw + ONES_ROWS, 2 * tq), F32), pltpu.VMEM((2, tq, 2 * tq), F32),
                        pltpu.VMEM((1, 2 * tq), F32), pltpu.VMEM((hw, 2 * tq), BF16)],
        compiler_params=_params("parallel", "parallel"),
        name="attn_prompt",
    )(lam, sub, qt, k, vt)


def _attn_sample_kernel(lam_ref, sub_ref, q_ref, ck_ref, cv_ref, nk_ref, nv_ref, o_ref, qq_scr, s_scr, m_scr,
                        acc_scr, *, n_heads, head_dim, n_cache_tiles, past_len, out_scale):
    j = pl.program_id(1)
    hw = 2 * head_dim
    rows = q_ref.shape[0]
    hr = 2 * rows

    @pl.when(j == 0)
    def _():
        m_scr[...] = jnp.full_like(m_scr, NEG)
        acc_scr[...] = jnp.zeros_like(acc_scr)
        for h in range(n_heads):
            qq_scr[h] = _stack_maps(q_ref[:, h * hw:(h + 1) * hw], head_dim)

    def softmax_and_values(n_keys, v_of_head):
        m_prev = m_scr[...]
        s_tiles = [s_scr[:, t * LANES:(t + 1) * LANES] for t in range(max(n_keys // LANES, 1))]
        if n_keys < LANES:
            m_cur = jnp.max(s_scr[:, 0:n_keys], axis=-1, keepdims=True)
        else:
            m_cur = functools.reduce(jnp.maximum, s_tiles).max(axis=-1, keepdims=True)
        m_new = jnp.maximum(m_prev, m_cur)
        m_scr[...] = m_new
        alpha = jnp.exp2(m_prev - m_new)
        if n_keys < LANES:
            p = jnp.exp2(s_scr[:, 0:n_keys] - m_new[:, 0:n_keys]).astype(BF16)
        else:
            p = jnp.concatenate([jnp.exp2(st - m_new) for st in s_tiles], axis=1).astype(BF16)
        ones = jnp.ones((n_keys, hw), BF16)
        for h in range(n_heads):
            rs = slice(h * hr, (h + 1) * hr)
            v1 = jnp.concatenate([v_of_head(h), ones], axis=1)
            a = alpha[rs]
            acc_scr[h] = jnp.concatenate([a, a], axis=1) * acc_scr[h] + _dot(p[rs], v1)

    @pl.when(j < n_cache_tiles)
    def _():
        for h in range(n_heads):
            s_scr[h * hr:(h + 1) * hr, :] = _dot(qq_scr[h], ck_ref[h * hw:(h + 1) * hw, :].astype(BF16))
        n_keys = ck_ref.shape[1]
        softmax_and_values(n_keys, lambda h: cv_ref[pl.ds(h, n_keys, stride=n_heads), :].astype(BF16))

    @pl.when(j == n_cache_tiles)
    def _():
        n_new = nk_ref.shape[0]
        row = lax.broadcasted_iota(jnp.int32, (hr, n_new), 0)
        row = jnp.where(row >= rows, row - rows, row)
        col = lax.broadcasted_iota(jnp.int32, (hr, n_new), 1)
        visible = ((past_len + col) >> CHUNK_SHIFT) <= ((past_len + row) >> CHUNK_SHIFT)
        for h in range(n_heads):
            s = _dot_nt(qq_scr[h], nk_ref[:, h * hw:(h + 1) * hw])
            s_scr[h * hr:(h + 1) * hr, 0:n_new] = jnp.where(visible, s, NEG)
        softmax_and_values(n_new, lambda h: nv_ref[:, h * hw:(h + 1) * hw])
        lam = lam_ref[...]
        sub = sub_ref[...]
        for h in range(n_heads):
            acc = acc_scr[h]
            o = acc[:, :hw] * (1.0 / acc[:, hw:])
            o = o[:rows] - lam * o[rows:]
            o_ref[:, h * hw:(h + 1) * hw] = (_rms(o, sub) * out_scale).astype(o_ref.dtype)


def _attn_sample_call(lam, sub, q, cache_k, cache_v, new_k, new_v, *, n_batch, dec_seq, past_len, n_heads,
                      head_dim, tk, out_scale):
    d_model = q.shape[1]
    hw = 2 * head_dim
    n_tiles = past_len // tk
    last = n_tiles - 1
    cache_map = lambda b, j: (b * n_tiles + jnp.minimum(j, last), 0)
    cache_kt_map = lambda b, j: (b, jnp.minimum(j, last))
    return pl.pallas_call(
        functools.partial(_attn_sample_kernel, n_heads=n_heads, head_dim=head_dim, n_cache_tiles=n_tiles,
                          past_len=past_len, out_scale=out_scale),
        grid=(n_batch, n_tiles + 1),
        in_specs=[
            pl.BlockSpec((1, LANES), lambda b, j: (0, 0)),
            pl.BlockSpec((1, hw), lambda b, j: (0, 0)),
            pl.BlockSpec((dec_seq, d_model), lambda b, j: (b, 0)),
            pl.BlockSpec((d_model, tk), cache_kt_map),
            pl.BlockSpec((tk * n_heads, hw), cache_map),
            pl.BlockSpec((dec_seq, d_model), lambda b, j: (b, 0)),
            pl.BlockSpec((dec_seq, d_model), lambda b, j: (b, 0)),
        ],
        out_specs=pl.BlockSpec((dec_seq, d_model), lambda b, j: (b, 0)),
        out_shape=jax.ShapeDtypeStruct(q.shape, BF16),
        scratch_shapes=[pltpu.VMEM((n_heads, 2 * dec_seq, hw), BF16),
                        pltpu.VMEM((n_heads * 2 * dec_seq, tk), F32),
                        pltpu.VMEM((n_heads * 2 * dec_seq, LANES), F32),
                        pltpu.VMEM((n_heads, 2 * dec_seq, 2 * hw), F32)],
        compiler_params=_params("parallel", "arbitrary"),
        name="attn_sample",
    )(lam, sub, q, cache_k, cache_v, new_k, new_v)


def _rope_tables(pos, head_dim, reps):
    half = head_dim // 2
    inv = ROPE_THETA ** (-jnp.arange(half, dtype=F32) / half)
    ang = pos.astype(F32)[:, None] * inv[None, :]
    cos = jnp.cos(ang)
    sin = jnp.sin(ang)
    n_groups = LANES // head_dim
    cos_t = jnp.tile(cos, (reps, 2 * n_groups))
    sin_t = jnp.tile(jnp.concatenate([-sin, sin], axis=-1), (reps, n_groups))
    return cos_t, sin_t


def _row_tile(m_rows, cap):
    t = cap
    while m_rows % t:
        t //= 2
    return t


def _trunk(x, pos, n_seq, h0, cache, p, prompt):
    m_rows, d_model = x.shape
    slen = m_rows // n_seq
    n_a = p['norm_ssm'].shape[0]
    depth = n_a + p['norm_attn'].shape[0]
    head_dim = p['k_norm'].shape[0]
    hw = 2 * head_dim
    n_heads = d_model // hw
    k_cols = n_heads * hw
    tm = _row_tile(m_rows, 512)
    tm_big = _row_tile(m_rows, 1024)
    d_ff = p['w_ffn_out'][0].shape[0]
    tf = 512 if d_ff % 512 == 0 else 256
    row = lambda a: a.astype(F32).reshape(1, -1)

    if prompt:
        seg_len = 32
        rope_reps = 1
    else:
        seg_len = slen
        rope_reps = tm // slen
    cos_t, sin_t = _rope_tables(pos, head_dim, rope_reps)

    new_h = []
    k_f32 = v_f32 = k_b = v_b = None
    for layer in range(depth):
        if layer < n_a:
            i = layer
            tabs = _ssm_tables(p['ssm_lam_re'][i], p['ssm_lam_im'][i], p['ssm_log_step'][i], p['ssm_b_re'][i],
                               p['ssm_b_im'][i], p['ssm_c_re'][i], p['ssm_c_im'][i], p['ssm_d'][i], seg_len)
            n_cb, _, sw = tabs['a_re'].shape
            if prompt:
                t_rows = SSM_SEG * seg_len
                gy, h_re, h_im = _s5_call(x, row(p['norm_ssm'][i]), tabs, None, seg_len=seg_len,
                                          chunks_per_seq=slen // t_rows)
                last = (jnp.arange(n_seq) + 1) * (slen // seg_len) - 1
                h_re, h_im = h_re[:, last], h_im[:, last]
            else:
                to_cb = lambda a: a.astype(F32).reshape(n_seq, n_cb, sw).transpose(1, 0, 2)
                gy, h_re, h_im = _s5_call(x, row(p['norm_ssm'][i]), tabs, (to_cb(h0[0][i]), to_cb(h0[1][i])),
                                          seg_len=seg_len, chunks_per_seq=1)
            from_cb = lambda a: a.transpose(1, 0, 2).reshape(n_seq, -1)
            new_h.append((from_cb(h_re), from_cb(h_im)))
            x = _glu_call(gy, p['w_glu'][i], x, tm=tm_big, tn=512)
        else:
            if layer == n_a:
                g_kv = row(p['norm_kv'])
                hn_k = jnp.tile(row(p['k_norm']), (1, LANES // head_dim))
                k_f32, k_b = _proj_call(x, g_kv, p['w_kv'], 0, k_cols,
                                        ("cols_f32" if prompt else "rows_f32", "rows_bf16"), tm=tm, tn=512,
                                        rope=(hn_k, cos_t, sin_t), head_dim=head_dim, name="k_proj", seq=slen)
                if prompt:
                    k_f32 = k_f32.reshape(n_seq, n_heads, 2, head_dim, slen).transpose(0, 4, 1, 2, 3)
                else:
                    k_f32 = k_f32.reshape(n_seq, slen, n_heads, 2, head_dim)
                v_f32, v_b = _proj_call(x, g_kv, p['w_kv'], k_cols, p['w_kv'].shape[1] - k_cols,
                                        ("rows_f32", "cols_bf16" if prompt else "rows_bf16"), tm=tm, tn=512,
                                        name="v_proj", seq=slen)
            j = layer - n_a
            lambda_init = 0.8 - 0.6 * math.exp(-0.3 * layer)
            hn_q = jnp.tile(row(p['q_norm'][j]), (1, LANES // head_dim))
            (q,) = _proj_call(x, row(p['norm_attn'][j]), p['w_q'][j], 0, k_cols,
                              ("cols_bf16" if prompt else "rows_bf16",), tm=tm, tn=512,
                              rope=(hn_q, cos_t, sin_t), scale=head_dim ** -0.5 * LOG2_E, head_dim=head_dim,
                              name="q_proj", seq=slen)
            lam = (jnp.exp(jnp.sum(p['lambda_q1'][j].astype(F32) * p['lambda_k1'][j].astype(F32)))
                   - jnp.exp(jnp.sum(p['lambda_q2'][j].astype(F32) * p['lambda_k2'][j].astype(F32)))
                   + lambda_init)
            sub = row(p['subln'][j])
            if prompt:
                o = _attn_prompt_call(jnp.full((1, tm), lam, F32), sub.reshape(-1, 1), q, k_b, v_b, n_batch=n_seq,
                                      seq=slen, n_heads=n_heads, head_dim=head_dim, out_scale=1.0 - lambda_init)
            else:
                lam = jnp.full((1, LANES), lam, F32)
                cache_k, cache_v = cache
                past_len = cache_k.shape[1]
                cache_kt = cache_k.transpose(0, 2, 3, 4, 1).reshape(n_seq * d_model, past_len)
                o = _attn_sample_call(lam, sub, q, cache_kt,
                                      cache_v.reshape(n_seq * past_len * n_heads, hw), k_b, v_b, n_batch=n_seq,
                                      dec_seq=slen, past_len=past_len, n_heads=n_heads, head_dim=head_dim,
                                      tk=512, out_scale=1.0 - lambda_init)
            x = _resmm_call(o, p['w_o'][j], x, tm=tm_big, tn=512)
        x = _ffn_call(x, row(p['norm_ffn'][layer]), p['w_ffn_in'][layer], p['w_ffn_out'][layer], tm=tm, tf=tf)
    return x, new_h, k_f32, v_f32


def kernel(x_prompt, x_sample, state_ssm_re, state_ssm_im, cache_k, cache_v, norm_ssm, ssm_lam_re, ssm_lam_im, ssm_log_step, ssm_b_re, ssm_b_im, ssm_c_re, ssm_c_im, ssm_d, w_glu, norm_kv, w_kv, k_norm, norm_attn, w_q, q_norm, lambda_q1, lambda_k1, lambda_q2, lambda_k2, subln, w_o, norm_ffn, w_ffn_in, w_ffn_out):
    n_b, seq, d_model = x_prompt.shape
    n_db, dec_seq, _ = x_sample.shape
    past_len = cache_k.shape[1]
    head_dim = k_norm.shape[0]
    n_heads = d_model // (2 * head_dim)
    n_groups, n_state = ssm_lam_re.shape[1:]
    assert ssm_b_re.shape[3] * n_groups == d_model and LANES % ssm_b_re.shape[3] == 0
    assert 2 * head_dim == LANES and seq % (SSM_SEG * 32) == 0 and n_db % SSM_SEG == 0
    assert past_len % CHUNK == 0 and dec_seq % 16 == 0

    p = dict(norm_ssm=norm_ssm, ssm_lam_re=ssm_lam_re, ssm_lam_im=ssm_lam_im, ssm_log_step=ssm_log_step,
             ssm_b_re=ssm_b_re, ssm_b_im=ssm_b_im, ssm_c_re=ssm_c_re, ssm_c_im=ssm_c_im, ssm_d=ssm_d,
             w_glu=w_glu.astype(BF16), norm_kv=norm_kv, w_kv=w_kv.astype(BF16), k_norm=k_norm,
             norm_attn=norm_attn, w_q=w_q.astype(BF16), q_norm=q_norm, lambda_q1=lambda_q1,
             lambda_k1=lambda_k1, lambda_q2=lambda_q2, lambda_k2=lambda_k2, subln=subln,
             w_o=w_o.astype(BF16), norm_ffn=norm_ffn, w_ffn_in=w_ffn_in.astype(BF16),
             w_ffn_out=w_ffn_out.astype(BF16))

    pos_p = jnp.arange(seq, dtype=jnp.int32)
    y_p, h_p, k_p, v_p = _trunk(x_prompt.reshape(n_b * seq, d_model), pos_p, n_b, None, None, p, True)
    pos_s = past_len + jnp.arange(dec_seq, dtype=jnp.int32)
    y_s, h_s, k_s, v_s = _trunk(x_sample.reshape(n_db * dec_seq, d_model), pos_s, n_db,
                                (state_ssm_re, state_ssm_im), (cache_k, cache_v), p, False)

    def states(hs, n_seq, part):
        return jnp.stack([h[part].reshape(n_seq, n_groups, n_state) for h in hs])

    return (y_p.reshape(n_b, seq, d_model), y_s.reshape(n_db, dec_seq, d_model),
            states(h_p, n_b, 0), states(h_p, n_b, 1),
            k_p, v_p.reshape(n_b, seq, n_heads, 2 * head_dim),
            states(h_s, n_db, 0), states(h_s, n_db, 1),
            k_s, v_s.reshape(n_db, dec_seq, n_heads, 2 * head_dim))
```

```python
import functools
import math

import jax
import jax.numpy as jnp
from jax import lax
from jax.experimental import pallas as pl
from jax.experimental.pallas import tpu as pltpu

CHUNK = 64
CHUNK_SHIFT = CHUNK.bit_length() - 1
EPS = 1e-6
ROPE_THETA = 10000.0
LANES = 128
SUBLANES = 8
SSM_SEG = SUBLANES
S5_TRIP = 16
S5_AHEAD = 2
S5_SLOTS = 4
VMEM_LIMIT = 56 * 1024 * 1024

F32 = jnp.float32
BF16 = jnp.bfloat16
NEG = -1e30
LOG2_E = math.log2(math.e)


def _dot(a, b):
    return jnp.dot(a, b, preferred_element_type=F32)


def _dot_nt(a, b):
    return lax.dot_general(a, b, (((1,), (1,)), ((), ())), preferred_element_type=F32)


def _rms(x, g):
    return x * lax.rsqrt(jnp.mean(x * x, axis=-1, keepdims=True) + EPS) * g


def _params(*sem):
    return pltpu.CompilerParams(dimension_semantics=sem, vmem_limit_bytes=VMEM_LIMIT)


def _cmul(ar, ai, br, bi):
    return ar * br - ai * bi, ar * bi + ai * br


def _cpow(ar, ai, n):
    rr, ri = None, None
    br, bi = ar, ai
    while n:
        if n & 1:
            rr, ri = (br, bi) if rr is None else _cmul(rr, ri, br, bi)
        n >>= 1
        if n:
            br, bi = _cmul(br, bi, br, bi)
    return rr, ri


def _s5_kernel(*refs, chain, seg_len, chunks_per_seq, n_cb, cw):
    if chain:
        (x_ref, g_ref, p_ref, pt_ref, b_ref, c_ref, are_ref, aim_ref, d_ref,
         gy_ref, hre_ref, him_ref, u_scr, bu_scr, gyp_scr, cre_scr, cim_scr) = refs
    else:
        (x_ref, g_ref, p_ref, pt_ref, b_ref, c_ref, are_ref, aim_ref, d_ref,
         h0re_ref, h0im_ref, gy_ref, hre_ref, him_ref, u_scr, bu_scr, gyp_scr) = refs
    sw = cw // 2

    u = _rms(x_ref[...], g_ref[...]).astype(BF16)
    up = _dot(p_ref[...], u).astype(BF16)
    for cb in range(n_cb):
        u_scr[cb] = up[:, cb * LANES:(cb + 1) * LANES]

    if chain:
        @pl.when(pl.program_id(0) % chunks_per_seq == 0)
        def _():
            cre_scr[...] = jnp.zeros_like(cre_scr)
            cim_scr[...] = jnp.zeros_like(cim_scr)

    row_id = lax.broadcasted_iota(jnp.int32, (SSM_SEG, sw), 0)

    def project_in(cb, slot):
        bu_scr[slot] = _dot(u_scr[cb], b_ref[cb])

    def scan_and_project_out(cb, slot):
        h_scr = bu_scr.at[slot]
        are = are_ref[cb]
        aim = aim_ref[cb]
        if chain:
            hre = jnp.zeros((SSM_SEG, sw), F32)
            him = jnp.zeros((SSM_SEG, sw), F32)
        else:
            hre = h0re_ref[cb]
            him = h0im_ref[cb]
        for k in range(seg_len):
            r0 = k * SSM_SEG
            nre = are * hre - aim * him + h_scr[r0:r0 + SSM_SEG, 0:sw]
            nim = are * him + aim * hre + h_scr[r0:r0 + SSM_SEG, sw:cw]
            hre, him = nre, nim
            h_scr[r0:r0 + SSM_SEG, 0:sw] = hre
            h_scr[r0:r0 + SSM_SEG, sw:cw] = him

        if chain:
            pre, pim = _cpow(are[0:1], aim[0:1], seg_len)
            cur_re = cre_scr[cb][SSM_SEG - 1:SSM_SEG]
            cur_im = cim_scr[cb][SSM_SEG - 1:SSM_SEG]
            prev_re = jnp.zeros((SSM_SEG, sw), F32)
            prev_im = jnp.zeros((SSM_SEG, sw), F32)
            end_re = jnp.zeros((SSM_SEG, sw), F32)
            end_im = jnp.zeros((SSM_SEG, sw), F32)
            for j in range(SSM_SEG):
                prev_re = jnp.where(row_id == j, cur_re, prev_re)
                prev_im = jnp.where(row_id == j, cur_im, prev_im)
                mre, mim = _cmul(pre, pim, cur_re, cur_im)
                cur_re = mre + hre[j:j + 1]
                cur_im = mim + him[j:j + 1]
                end_re = jnp.where(row_id == j, cur_re, end_re)
                end_im = jnp.where(row_id == j, cur_im, end_im)
            cre_scr[cb] = end_re
            cim_scr[cb] = end_im
            hre_ref[cb] = end_re
            him_ref[cb] = end_im
            qre, qim = _cmul(are, aim, prev_re, prev_im)
            for k in range(seg_len):
                r0 = k * SSM_SEG
                h_scr[r0:r0 + SSM_SEG, 0:sw] = h_scr[r0:r0 + SSM_SEG, 0:sw] + qre
                h_scr[r0:r0 + SSM_SEG, sw:cw] = h_scr[r0:r0 + SSM_SEG, sw:cw] + qim
                if k + 1 < seg_len:
                    qre, qim = _cmul(are, aim, qre, qim)
        else:
            hre_ref[cb] = hre
            him_ref[cb] = him

        y = _dot(h_scr[...].astype(BF16), c_ref[cb])
        y = y + d_ref[cb] * u_scr[cb].astype(F32)
        gyp_scr[cb] = jax.nn.gelu(y, approximate=True).astype(BF16)

    def trip(t, carry):
        cb0 = t * S5_TRIP
        for c in range(min(S5_AHEAD, S5_TRIP)):
            project_in(cb0 + c, c % S5_SLOTS)
        for c in range(S5_TRIP):
            if c + S5_AHEAD < S5_TRIP:
                project_in(cb0 + c + S5_AHEAD, (c + S5_AHEAD) % S5_SLOTS)
            scan_and_project_out(cb0 + c, c % S5_SLOTS)
        return carry

    lax.fori_loop(0, n_cb // S5_TRIP, trip, 0)

    perm_t = pt_ref[...]
    for cb in range(n_cb):
        gy_ref[:, cb * LANES:(cb + 1) * LANES] = _dot(perm_t, gyp_scr[cb]).astype(BF16)


def _s5_call(x, g, ssm, h0, *, seg_len, chunks_per_seq):
    m_rows, d_model = x.shape
    chain = h0 is None
    t_rows = SSM_SEG * seg_len
    n_chunks = m_rows // t_rows
    n_cb, _, cw = ssm['b'].shape
    sw = cw // 2
    full3 = lambda i: (0, 0, 0)
    full2 = lambda i: (0, 0)
    in_specs = [
        pl.BlockSpec((t_rows, d_model), lambda i: (i, 0)),
        pl.BlockSpec((1, d_model), full2),
        pl.BlockSpec((t_rows, t_rows), full2),
        pl.BlockSpec((t_rows, t_rows), full2),
        pl.BlockSpec((n_cb, LANES, cw), full3),
        pl.BlockSpec((n_cb, cw, LANES), full3),
        pl.BlockSpec((n_cb, SSM_SEG, sw), full3),
        pl.BlockSpec((n_cb, SSM_SEG, sw), full3),
        pl.BlockSpec((n_cb, 1, LANES), full3),
    ]
    args = [x, g, ssm['perm'], ssm['perm_t'], ssm['b'], ssm['c'], ssm['a_re'], ssm['a_im'], ssm['d']]
    if not chain:
        in_specs += [pl.BlockSpec((n_cb, SSM_SEG, sw), lambda i: (0, i, 0))] * 2
        args += [h0[0], h0[1]]
    scratch = [
        pltpu.VMEM((n_cb, t_rows, LANES), BF16),
        pltpu.VMEM((S5_SLOTS, t_rows, cw), F32),
        pltpu.VMEM((n_cb, t_rows, LANES), BF16),
    ]
    if chain:
        scratch += [pltpu.VMEM((n_cb, SSM_SEG, sw), F32)] * 2
    state_shape = jax.ShapeDtypeStruct((n_cb, n_chunks * SSM_SEG, sw), F32)
    return pl.pallas_call(
        functools.partial(_s5_kernel, chain=chain, seg_len=seg_len, chunks_per_seq=chunks_per_seq,
                          n_cb=n_cb, cw=cw),
        grid=(n_chunks,),
        in_specs=in_specs,
        out_specs=[
            pl.BlockSpec((t_rows, d_model), lambda i: (i, 0)),
            pl.BlockSpec((n_cb, SSM_SEG, sw), lambda i: (0, i, 0)),
            pl.BlockSpec((n_cb, SSM_SEG, sw), lambda i: (0, i, 0)),
        ],
        out_shape=[jax.ShapeDtypeStruct((m_rows, d_model), BF16), state_shape, state_shape],
        scratch_shapes=scratch,
        compiler_params=_params("arbitrary"),
        name="s5_chain" if chain else "s5_step",
    )(*args)


def _ssm_tables(lam_re, lam_im, log_step, b_re, b_im, c_re, c_im, d, seg_len):
    n_groups, n_state, n_chan = b_re.shape
    gpb = LANES // n_chan
    n_cb = n_groups // gpb
    sw = gpb * n_state
    lam_re = lam_re.astype(F32)
    lam_im = lam_im.astype(F32)
    dt = jnp.exp(log_step.astype(F32))[:, None]
    mag = jnp.exp(lam_re * dt)
    a_re = mag * jnp.cos(lam_im * dt)
    a_im = mag * jnp.sin(lam_im * dt)
    den = lam_re * lam_re + lam_im * lam_im
    q_re = ((a_re - 1.0) * lam_re + a_im * lam_im) / den
    q_im = (a_im * lam_re - (a_re - 1.0) * lam_im) / den
    bb_re = q_re[..., None] * b_re.astype(F32) - q_im[..., None] * b_im.astype(F32)
    bb_im = q_re[..., None] * b_im.astype(F32) + q_im[..., None] * b_re.astype(F32)
    eye = jnp.eye(gpb, dtype=F32)

    def bmat(bb):
        t = bb.reshape(n_cb, gpb, n_state, n_chan)
        t = jnp.einsum('cgnp,gh->cgphn', t, eye)
        return t.reshape(n_cb, gpb * n_chan, sw)

    def cmat(cc):
        t = cc.astype(F32).reshape(n_cb, gpb, n_chan, n_state)
        t = jnp.einsum('cgpn,gh->cgnhp', t, eye)
        return t.reshape(n_cb, sw, gpb * n_chan)

    b_full = jnp.concatenate([bmat(bb_re), bmat(bb_im)], axis=-1).astype(BF16)
    c_full = jnp.concatenate([cmat(c_re), -cmat(c_im)], axis=1).astype(BF16)
    rep = lambda a: jnp.broadcast_to(a.reshape(n_cb, 1, sw), (n_cb, SSM_SEG, sw))
    t_rows = SSM_SEG * seg_len
    r = jnp.arange(t_rows)
    src = (r % SSM_SEG) * seg_len + r // SSM_SEG
    perm = (src[:, None] == jnp.arange(t_rows)[None, :]).astype(BF16)
    return dict(b=b_full, c=c_full, a_re=rep(a_re), a_im=rep(a_im),
                d=d.astype(F32).reshape(n_cb, 1, LANES), perm=perm, perm_t=perm.T)


def _glu_kernel(a_ref, wv_ref, wg_ref, x_ref, o_ref):
    a = a_ref[...]
    zv = _dot(a, wv_ref[...])
    zg = _dot(a, wg_ref[...])
    o_ref[...] = x_ref[...] + zv * jax.nn.sigmoid(zg)


def _glu_call(a, w, x, *, tm, tn):
    m_rows, d_in = a.shape
    d_out = x.shape[1]
    nj = d_out // tn
    return pl.pallas_call(
        _glu_kernel,
        grid=(m_rows // tm, nj),
        in_specs=[
            pl.BlockSpec((tm, d_in), lambda i, j: (i, 0)),
            pl.BlockSpec((d_in, tn), lambda i, j: (0, j)),
            pl.BlockSpec((d_in, tn), lambda i, j: (0, j + nj)),
            pl.BlockSpec((tm, tn), lambda i, j: (i, j)),
        ],
        out_specs=pl.BlockSpec((tm, tn), lambda i, j: (i, j)),
        out_shape=jax.ShapeDtypeStruct((m_rows, d_out), F32),
        compiler_params=_params("parallel", "arbitrary"),
        name="glu",
    )(a, w, w, x)


def _ffn_kernel(x_ref, g_ref, wa_ref, wb_ref, wo_ref, o_ref, xn_scr):
    @pl.when(pl.program_id(1) == 0)
    def _():
        x = x_ref[...]
        xn_scr[...] = _rms(x, g_ref[...]).astype(BF16)
        o_ref[...] = x

    xn = xn_scr[...]
    a = _dot(xn, wa_ref[...])
    b = _dot(xn, wb_ref[...])
    h = (a * jax.nn.sigmoid(a) * b).astype(BF16)
    o_ref[...] += _dot(h, wo_ref[...])


def _ffn_call(x, g, w_in, w_out, *, tm, tf):
    m_rows, d_model = x.shape
    d_ff = w_out.shape[0]
    nj = d_ff // tf
    return pl.pallas_call(
        _ffn_kernel,
        grid=(m_rows // tm, nj),
        in_specs=[
            pl.BlockSpec((tm, d_model), lambda i, j: (i, 0)),
            pl.BlockSpec((1, d_model), lambda i, j: (0, 0)),
            pl.BlockSpec((d_model, tf), lambda i, j: (0, j)),
            pl.BlockSpec((d_model, tf), lambda i, j: (0, j + nj)),
            pl.BlockSpec((tf, d_model), lambda i, j: (j, 0)),
        ],
        out_specs=pl.BlockSpec((tm, d_model), lambda i, j: (i, 0)),
        out_shape=jax.ShapeDtypeStruct((m_rows, d_model), F32),
        scratch_shapes=[pltpu.VMEM((tm, d_model), BF16)],
        compiler_params=_params("parallel", "arbitrary"),
        name="ffn",
    )(x, g, w_in, w_in, w_out)


def _proj_kernel(*refs, head_norm, scale, outs, head_dim, tn):
    n_out = len(outs)
    if head_norm:
        x_ref, g_ref, w_ref, hn_ref, cos_ref, sin_ref = refs[:6]
        out_refs = refs[6:6 + n_out]
    else:
        x_ref, g_ref, w_ref = refs[:3]
        out_refs = refs[3:3 + n_out]
    xn = _rms(x_ref[...], g_ref[...]).astype(BF16)
    tm = xn.shape[0]
    n_tiles = w_ref.shape[1] // tn
    if head_norm:
        half = head_dim // 2
        hd_shift = head_dim.bit_length() - 1
        gi = lax.broadcasted_iota(jnp.int32, (LANES, LANES), 0) >> hd_shift
        gj = lax.broadcasted_iota(jnp.int32, (LANES, LANES), 1) >> hd_shift
        gmat = (gi == gj).astype(BF16)
        lane = lax.broadcasted_iota(jnp.int32, (tm, LANES), 1)
        first_half = (lane & (head_dim - 1)) < half
        cos = cos_ref[...]
        sin = sin_ref[...]
        hn = hn_ref[...]
    need_t = any(kind.startswith("cols") for kind in outs)

    def matmul(t):
        return _dot(xn, w_ref[:, t * tn:(t + 1) * tn])

    def finish(t, y):
        for s in range(tn // LANES):
            cs = slice(t * tn + s * LANES, t * tn + (s + 1) * LANES)
            r = y[:, s * LANES:(s + 1) * LANES]
            if head_norm:
                ss = _dot((r * r).astype(BF16), gmat)
                yn = r * lax.rsqrt(ss * (1.0 / head_dim) + EPS) * hn
                partner = jnp.where(first_half, pltpu.roll(yn, LANES - half, 1), pltpu.roll(yn, half, 1))
                r = yn * cos + partner * sin
                if scale != 1.0:
                    r = r * scale
            rt = r.T if need_t else None
            for kind, o_ref in zip(outs, out_refs):
                if kind == "cols_f32":
                    o_ref[cs, :] = rt
                elif kind == "cols_bf16":
                    o_ref[0, 0, cs, :] = rt.astype(BF16)
                else:
                    o_ref[:, cs] = r.astype(o_ref.dtype)

    pending = matmul(0)
    for t in range(1, n_tiles):
        nxt = matmul(t)
        finish(t - 1, pending)
        pending = nxt
    finish(n_tiles - 1, pending)


def _proj_call(x, g, w, col0, n_cols, outs, *, tm, tn, rope=None, scale=1.0, head_dim=64, name="proj", seq=None):
    m_rows, d_model = x.shape
    head_norm = rope is not None
    out_specs, out_shape = [], []
    for kind in outs:
        if kind == "cols_f32":
            tiles = seq // tm
            out_specs.append(pl.BlockSpec((n_cols, tm), lambda i: (i // tiles, i % tiles)))
            out_shape.append(jax.ShapeDtypeStruct((m_rows // seq * n_cols, seq), F32))
        elif kind == "cols_bf16":
            tiles = seq // tm
            out_specs.append(pl.BlockSpec((1, 1, n_cols, tm), lambda i: (i // tiles, i % tiles, 0, 0)))
            out_shape.append(jax.ShapeDtypeStruct((m_rows // seq, tiles, n_cols, tm), BF16))
        else:
            out_specs.append(pl.BlockSpec((tm, n_cols), lambda i: (i, 0)))
            out_shape.append(jax.ShapeDtypeStruct((m_rows, n_cols), F32 if kind == "rows_f32" else BF16))
    in_specs = [
        pl.BlockSpec((tm, d_model), lambda i: (i, 0)),
        pl.BlockSpec((1, d_model), lambda i: (0, 0)),
        pl.BlockSpec((d_model, n_cols), lambda i: (0, col0 // n_cols)),
    ]
    args = [x, g, w]
    if head_norm:
        hn, cos, sin = rope
        nrep = cos.shape[0] // tm
        in_specs += [
            pl.BlockSpec((1, LANES), lambda i: (0, 0)),
            pl.BlockSpec((tm, LANES), lambda i: (i % nrep, 0)),
            pl.BlockSpec((tm, LANES), lambda i: (i % nrep, 0)),
        ]
        args += [hn, cos, sin]
    return pl.pallas_call(
        functools.partial(_proj_kernel, head_norm=head_norm, scale=scale, outs=tuple(outs), head_dim=head_dim,
                          tn=tn),
        grid=(m_rows // tm,),
        in_specs=in_specs,
        out_specs=out_specs,
        out_shape=out_shape,
        compiler_params=_params("parallel"),
        name=name,
    )(*args)


def _resmm_kernel(a_ref, w_ref, x_ref, o_ref):
    o_ref[...] = x_ref[...] + _dot(a_ref[...], w_ref[...])


def _resmm_call(a, w, x, *, tm, tn):
    m_rows, d_in = a.shape
    d_out = w.shape[1]
    return pl.pallas_call(
        _resmm_kernel,
        grid=(m_rows // tm, d_out // tn),
        in_specs=[
            pl.BlockSpec((tm, d_in), lambda i, j: (i, 0)),
            pl.BlockSpec((d_in, tn), lambda i, j: (0, j)),
            pl.BlockSpec((tm, tn), lambda i, j: (i, j)),
        ],
        out_specs=pl.BlockSpec((tm, tn), lambda i, j: (i, j)),
        out_shape=jax.ShapeDtypeStruct((m_rows, d_out), F32),
        compiler_params=_params("parallel", "arbitrary"),
        name="out_proj",
    )(a, w, x)


def _stack_maps(q, head_dim):
    lane = lax.broadcasted_iota(jnp.int32, q.shape, 1)
    zero = jnp.zeros_like(q)
    return jnp.concatenate([jnp.where(lane < head_dim, q, zero), jnp.where(lane >= head_dim, q, zero)], axis=0)


ONES_ROWS = 16


def _attn_prompt_kernel(lam_ref, sub_ref, qt_ref, k_ref, vt_ref, o_ref, acc_scr, st_scr, m_scr, qq_scr,
                        *, tq, head_dim, out_scale):
    hw = 2 * head_dim
    nq = qt_ref.shape[1]
    ones = jnp.ones((ONES_ROWS, tq), BF16)

    def stage_queries(qi):
        qt = qt_ref[0, qi]
        feat = lax.broadcasted_iota(jnp.int32, qt.shape, 0)
        zero = jnp.zeros_like(qt)
        qq_scr[...] = jnp.concatenate(
            [jnp.where(feat < head_dim, qt, zero), jnp.where(feat >= head_dim, qt, zero)], axis=1)

    def scores(kj, slot):
        start = pl.multiple_of(kj * tq, tq)
        st_scr[slot] = _dot(k_ref[pl.ds(start, tq), :], qq_scr[...])

    def softmax_values(kj, slot, masked):
        st = st_scr[slot]
        if masked:
            kpos = lax.broadcasted_iota(jnp.int32, st.shape, 0)
            qpos = lax.broadcasted_iota(jnp.int32, st.shape, 1)
            qpos = jnp.where(qpos >= tq, qpos - tq, qpos)
            st = jnp.where((kpos >> CHUNK_SHIFT) <= (qpos >> CHUNK_SHIFT), st, NEG)
        m_prev = m_scr[...]
        m_new = jnp.maximum(m_prev, jnp.max(st, axis=0, keepdims=True))
        m_scr[...] = m_new
        alpha = jnp.exp2(m_prev - m_new)
        pt = jnp.exp2(st - m_new).astype(BF16)
        vt1 = jnp.concatenate([vt_ref[0, kj], ones], axis=0)
        acc_scr[...] = alpha * acc_scr[...] + _dot(vt1, pt)

    def two_tiles(t, carry):
        scores(2 * t + 1, 1)
        softmax_values(2 * t, 0, False)
        scores(2 * t + 2, 0)
        softmax_values(2 * t + 1, 1, False)
        return carry

    def query_tile(qi, carry):
        acc_scr[...] = jnp.zeros_like(acc_scr)
        m_scr[...] = jnp.full_like(m_scr, NEG)
        lax.fori_loop(0, qi // 2, two_tiles, 0)

        @pl.when(qi % 2 == 0)
        def _():
            softmax_values(qi, 0, True)

        @pl.when(qi % 2 == 1)
        def _():
            scores(qi, 1)
            softmax_values(qi - 1, 0, False)
            softmax_values(qi, 1, True)

        stage_queries(jnp.minimum(qi + 1, nq - 1))
        scores(0, 0)
        acc = acc_scr[...]
        ot = acc[:hw] * (1.0 / acc[hw:hw + 1])
        ot = ot[:, :tq] - lam_ref[...] * ot[:, tq:]
        ot = ot * lax.rsqrt(jnp.mean(ot * ot, axis=0, keepdims=True) + EPS) * (sub_ref[...] * out_scale)
        o_ref[pl.ds(pl.multiple_of(qi * tq, tq), tq), :] = ot.T.astype(o_ref.dtype)
        return carry

    stage_queries(0)
    scores(0, 0)
    lax.fori_loop(0, nq, query_tile, 0)


def _attn_prompt_call(lam, sub, qt, k, vt, *, n_batch, seq, n_heads, head_dim, out_scale):
    tq = qt.shape[3]
    nq = seq // tq
    hw = 2 * head_dim
    return pl.pallas_call(
        functools.partial(_attn_prompt_kernel, tq=tq, head_dim=head_dim, out_scale=out_scale),
        grid=(n_batch, n_heads),
        in_specs=[
            pl.BlockSpec((1, tq), lambda b, h: (0, 0)),
            pl.BlockSpec((hw, 1), lambda b, h: (0, 0)),
            pl.BlockSpec((1, nq, hw, tq), lambda b, h: (b, 0, h, 0)),
            pl.BlockSpec((seq, hw), lambda b, h: (b, h)),
            pl.BlockSpec((1, nq, hw, tq), lambda b, h: (b, 0, h, 0)),
        ],
        out_specs=pl.BlockSpec((seq, hw), lambda b, h: (b, h)),
        out_shape=jax.ShapeDtypeStruct(k.shape, BF16),
        scratch_shapes=[pltpu.VMEM((hw + ONES_ROWS, 2 * tq), F32), pltpu.VMEM((2, tq, 2 * tq), F32),
                        pltpu.VMEM((1, 2 * tq), F32), pltpu.VMEM((hw, 2 * tq), BF16)],
        compiler_params=_params("parallel", "parallel"),
        name="attn_prompt",
    )(lam, sub, qt, k, vt)


def _attn_sample_kernel(lam_ref, sub_ref, q_ref, ck_ref, cv_ref, nk_ref, nv_ref, o_ref, qq_scr, s_scr, m_scr,
                        acc_scr, *, n_heads, head_dim, n_cache_tiles, past_len, out_scale):
    j = pl.program_id(1)
    hw = 2 * head_dim
    rows = q_ref.shape[0]
    hr = 2 * rows

    @pl.when(j == 0)
    def _():
        m_scr[...] = jnp.full_like(m_scr, NEG)
        acc_scr[...] = jnp.zeros_like(acc_scr)
        for h in range(n_heads):
            qq_scr[h] = _stack_maps(q_ref[:, h * hw:(h + 1) * hw], head_dim)

    def softmax_and_values(n_keys, v_of_head):
        m_prev = m_scr[...]
        s_tiles = [s_scr[:, t * LANES:(t + 1) * LANES] for t in range(max(n_keys // LANES, 1))]
        if n_keys < LANES:
            m_cur = jnp.max(s_scr[:, 0:n_keys], axis=-1, keepdims=True)
        else:
            m_cur = functools.reduce(jnp.maximum, s_tiles).max(axis=-1, keepdims=True)
        m_new = jnp.maximum(m_prev, m_cur)
        m_scr[...] = m_new
        alpha = jnp.exp2(m_prev - m_new)
        if n_keys < LANES:
            p = jnp.exp2(s_scr[:, 0:n_keys] - m_new[:, 0:n_keys]).astype(BF16)
        else:
            p = jnp.concatenate([jnp.exp2(st - m_new) for st in s_tiles], axis=1).astype(BF16)
        ones = jnp.ones((n_keys, hw), BF16)
        for h in range(n_heads):
            rs = slice(h * hr, (h + 1) * hr)
            v1 = jnp.concatenate([v_of_head(h), ones], axis=1)
            a = alpha[rs]
            acc_scr[h] = jnp.concatenate([a, a], axis=1) * acc_scr[h] + _dot(p[rs], v1)

    @pl.when(j < n_cache_tiles)
    def _():
        for h in range(n_heads):
            s_scr[h * hr:(h + 1) * hr, :] = _dot(qq_scr[h], ck_ref[h * hw:(h + 1) * hw, :].astype(BF16))
        n_keys = ck_ref.shape[1]
        softmax_and_values(n_keys, lambda h: cv_ref[pl.ds(h, n_keys, stride=n_heads), :].astype(BF16))

    @pl.when(j == n_cache_tiles)
    def _():
        n_new = nk_ref.shape[0]
        row = lax.broadcasted_iota(jnp.int32, (hr, n_new), 0)
        row = jnp.where(row >= rows, row - rows, row)
        col = lax.broadcasted_iota(jnp.int32, (hr, n_new), 1)
        visible = ((past_len + col) >> CHUNK_SHIFT) <= ((past_len + row) >> CHUNK_SHIFT)
        for h in range(n_heads):
            s = _dot_nt(qq_scr[h], nk_ref[:, h * hw:(h + 1) * hw])
            s_scr[h * hr:(h + 1) * hr, 0:n_new] = jnp.where(visible, s, NEG)
        softmax_and_values(n_new, lambda h: nv_ref[:, h * hw:(h + 1) * hw])
        lam = lam_ref[...]
        sub = sub_ref[...]
        for h in range(n_heads):
            acc = acc_scr[h]
            o = acc[:, :hw] * (1.0 / acc[:, hw:])
            o = o[:rows] - lam * o[rows:]
            o_ref[:, h * hw:(h + 1) * hw] = (_rms(o, sub) * out_scale).astype(o_ref.dtype)


def _attn_sample_call(lam, sub, q, cache_k, cache_v, new_k, new_v, *, n_batch, dec_seq, past_len, n_heads,
                      head_dim, tk, out_scale):
    d_model = q.shape[1]
    hw = 2 * head_dim
    n_tiles = past_len // tk
    last = n_tiles - 1
    cache_map = lambda b, j: (b * n_tiles + jnp.minimum(j, last), 0)
    cache_kt_map = lambda b, j: (b, jnp.minimum(j, last))
    return pl.pallas_call(
        functools.partial(_attn_sample_kernel, n_heads=n_heads, head_dim=head_dim, n_cache_tiles=n_tiles,
                          past_len=past_len, out_scale=out_scale),
        grid=(n_batch, n_tiles + 1),
        in_specs=[
            pl.BlockSpec((1, LANES), lambda b, j: (0, 0)),
            pl.BlockSpec((1, hw), lambda b, j: (0, 0)),
            pl.BlockSpec((dec_seq, d_model), lambda b, j: (b, 0)),
            pl.BlockSpec((d_model, tk), cache_kt_map),
            pl.BlockSpec((tk * n_heads, hw), cache_map),
            pl.BlockSpec((dec_seq, d_model), lambda b, j: (b, 0)),
            pl.BlockSpec((dec_seq, d_model), lambda b, j: (b, 0)),
        ],
        out_specs=pl.BlockSpec((dec_seq, d_model), lambda b, j: (b, 0)),
        out_shape=jax.ShapeDtypeStruct(q.shape, BF16),
        scratch_shapes=[pltpu.VMEM((n_heads, 2 * dec_seq, hw), BF16),
                        pltpu.VMEM((n_heads * 2 * dec_seq, tk), F32),
                        pltpu.VMEM((n_heads * 2 * dec_seq, LANES), F32),
                        pltpu.VMEM((n_heads, 2 * dec_seq, 2 * hw), F32)],
        compiler_params=_params("parallel", "arbitrary"),
        name="attn_sample",
    )(lam, sub, q, cache_k, cache_v, new_k, new_v)


def _rope_tables(pos, head_dim, reps):
    half = head_dim // 2
    inv = ROPE_THETA ** (-jnp.arange(half, dtype=F32) / half)
    ang = pos.astype(F32)[:, None] * inv[None, :]
    cos = jnp.cos(ang)
    sin = jnp.sin(ang)
    n_groups = LANES // head_dim
    cos_t = jnp.tile(cos, (reps, 2 * n_groups))
    sin_t = jnp.tile(jnp.concatenate([-sin, sin], axis=-1), (reps, n_groups))
    return cos_t, sin_t


def _row_tile(m_rows, cap):
    t = cap
    while m_rows % t:
        t //= 2
    return t


def _trunk(x, pos, n_seq, h0, cache, p, prompt):
    m_rows, d_model = x.shape
    slen = m_rows // n_seq
    n_a = p['norm_ssm'].shape[0]
    depth = n_a + p['norm_attn'].shape[0]
    head_dim = p['k_norm'].shape[0]
    hw = 2 * head_dim
    n_heads = d_model // hw
    k_cols = n_heads * hw
    tm = _row_tile(m_rows, 512)
    tm_big = _row_tile(m_rows, 1024)
    d_ff = p['w_ffn_out'][0].shape[0]
    tf = 512 if d_ff % 512 == 0 else 256
    row = lambda a: a.astype(F32).reshape(1, -1)

    if prompt:
        seg_len = 32
        rope_reps = 1
    else:
        seg_len = slen
        rope_reps = tm // slen
    cos_t, sin_t = _rope_tables(pos, head_dim, rope_reps)

    new_h = []
    k_f32 = v_f32 = k_b = v_b = None
    for layer in range(depth):
        if layer < n_a:
            i = layer
            tabs = _ssm_tables(p['ssm_lam_re'][i], p['ssm_lam_im'][i], p['ssm_log_step'][i], p['ssm_b_re'][i],
                               p['ssm_b_im'][i], p['ssm_c_re'][i], p['ssm_c_im'][i], p['ssm_d'][i], seg_len)
            n_cb, _, sw = tabs['a_re'].shape
            if prompt:
                t_rows = SSM_SEG * seg_len
                gy, h_re, h_im = _s5_call(x, row(p['norm_ssm'][i]), tabs, None, seg_len=seg_len,
                                          chunks_per_seq=slen // t_rows)
                last = (jnp.arange(n_seq) + 1) * (slen // seg_len) - 1
                h_re, h_im = h_re[:, last], h_im[:, last]
            else:
                to_cb = lambda a: a.astype(F32).reshape(n_seq, n_cb, sw).transpose(1, 0, 2)
                gy, h_re, h_im = _s5_call(x, row(p['norm_ssm'][i]), tabs, (to_cb(h0[0][i]), to_cb(h0[1][i])),
                                          seg_len=seg_len, chunks_per_seq=1)
            from_cb = lambda a: a.transpose(1, 0, 2).reshape(n_seq, -1)
            new_h.append((from_cb(h_re), from_cb(h_im)))
            x = _glu_call(gy, p['w_glu'][i], x, tm=tm_big, tn=512)
        else:
            if layer == n_a:
                g_kv = row(p['norm_kv'])
                hn_k = jnp.tile(row(p['k_norm']), (1, LANES // head_dim))
                k_f32, k_b = _proj_call(x, g_kv, p['w_kv'], 0, k_cols,
                                        ("cols_f32" if prompt else "rows_f32", "rows_bf16"), tm=tm, tn=512,
                                        rope=(hn_k, cos_t, sin_t), head_dim=head_dim, name="k_proj", seq=slen)
                if prompt:
                    k_f32 = k_f32.reshape(n_seq, n_heads, 2, head_dim, slen).transpose(0, 4, 1, 2, 3)
                else:
                    k_f32 = k_f32.reshape(n_seq, slen, n_heads, 2, head_dim)
                v_f32, v_b = _proj_call(x, g_kv, p['w_kv'], k_cols, p['w_kv'].shape[1] - k_cols,
                                        ("rows_f32", "cols_bf16" if prompt else "rows_bf16"), tm=tm, tn=512,
                                        name="v_proj", seq=slen)
            j = layer - n_a
            lambda_init = 0.8 - 0.6 * math.exp(-0.3 * layer)
            hn_q = jnp.tile(row(p['q_norm'][j]), (1, LANES // head_dim))
            (q,) = _proj_call(x, row(p['norm_attn'][j]), p['w_q'][j], 0, k_cols,
                              ("cols_bf16" if prompt else "rows_bf16",), tm=tm, tn=512,
                              rope=(hn_q, cos_t, sin_t), scale=head_dim ** -0.5 * LOG2_E, head_dim=head_dim,
                              name="q_proj", seq=slen)
            lam = (jnp.exp(jnp.sum(p['lambda_q1'][j].astype(F32) * p['lambda_k1'][j].astype(F32)))
                   - jnp.exp(jnp.sum(p['lambda_q2'][j].astype(F32) * p['lambda_k2'][j].astype(F32)))
                   + lambda_init)
            sub = row(p['subln'][j])
            if prompt:
                o = _attn_prompt_call(jnp.full((1, tm), lam, F32), sub.reshape(-1, 1), q, k_b, v_b, n_batch=n_seq,
                                      seq=slen, n_heads=n_heads, head_dim=head_dim, out_scale=1.0 - lambda_init)
            else:
                lam = jnp.full((1, LANES), lam, F32)
                cache_k, cache_v = cache
                past_len = cache_k.shape[1]
                cache_kt = cache_k.transpose(0, 2, 3, 4, 1).reshape(n_seq * d_model, past_len)
                o = _attn_sample_call(lam, sub, q, cache_kt,
                                      cache_v.reshape(n_seq * past_len * n_heads, hw), k_b, v_b, n_batch=n_seq,
                                      dec_seq=slen, past_len=past_len, n_heads=n_heads, head_dim=head_dim,
                                      tk=512, out_scale=1.0 - lambda_init)
            x = _resmm_call(o, p['w_o'][j], x, tm=tm_big, tn=512)
        x = _ffn_call(x, row(p['norm_ffn'][layer]), p['w_ffn_in'][layer], p['w_ffn_out'][layer], tm=tm, tf=tf)
    return x, new_h, k_f32, v_f32


def kernel(x_prompt, x_sample, state_ssm_re, state_ssm_im, cache_k, cache_v, norm_ssm, ssm_lam_re, ssm_lam_im, ssm_log_step, ssm_b_re, ssm_b_im, ssm_c_re, ssm_c_im, ssm_d, w_glu, norm_kv, w_kv, k_norm, norm_attn, w_q, q_norm, lambda_q1, lambda_k1, lambda_q2, lambda_k2, subln, w_o, norm_ffn, w_ffn_in, w_ffn_out):
    n_b, seq, d_model = x_prompt.shape
    n_db, dec_seq, _ = x_sample.shape
    past_len = cache_k.shape[1]
    head_dim = k_norm.shape[0]
    n_heads = d_model // (2 * head_dim)
    n_groups, n_state = ssm_lam_re.shape[1:]
    assert ssm_b_re.shape[3] * n_groups == d_model and LANES % ssm_b_re.shape[3] == 0
    assert 2 * head_dim == LANES and seq % (SSM_SEG * 32) == 0 and n_db % SSM_SEG == 0
    assert past_len % CHUNK == 0 and dec_seq % 16 == 0

    p = dict(norm_ssm=norm_ssm, ssm_lam_re=ssm_lam_re, ssm_lam_im=ssm_lam_im, ssm_log_step=ssm_log_step,
             ssm_b_re=ssm_b_re, ssm_b_im=ssm_b_im, ssm_c_re=ssm_c_re, ssm_c_im=ssm_c_im, ssm_d=ssm_d,
             w_glu=w_glu.astype(BF16), norm_kv=norm_kv, w_kv=w_kv.astype(BF16), k_norm=k_norm,
             norm_attn=norm_attn, w_q=w_q.astype(BF16), q_norm=q_norm, lambda_q1=lambda_q1,
             lambda_k1=lambda_k1, lambda_q2=lambda_q2, lambda_k2=lambda_k2, subln=subln,
             w_o=w_o.astype(BF16), norm_ffn=norm_ffn, w_ffn_in=w_ffn_in.astype(BF16),
             w_ffn_out=w_ffn_out.astype(BF16))

    pos_p = jnp.arange(seq, dtype=jnp.int32)
    y_p, h_p, k_p, v_p = _trunk(x_prompt.reshape(n_b * seq, d_model), pos_p, n_b, None, None, p, True)
    pos_s = past_len + jnp.arange(dec_seq, dtype=jnp.int32)
    y_s, h_s, k_s, v_s = _trunk(x_sample.reshape(n_db * dec_seq, d_model), pos_s, n_db,
                                (state_ssm_re, state_ssm_im), (cache_k, cache_v), p, False)

    def states(hs, n_seq, part):
        return jnp.stack([h[part].reshape(n_seq, n_groups, n_state) for h in hs])

    return (y_p.reshape(n_b, seq, d_model), y_s.reshape(n_db, dec_seq, d_model),
            states(h_p, n_b, 0), states(h_p, n_b, 1),
            k_p, v_p.reshape(n_b, seq, n_heads, 2 * head_dim),
            states(h_s, n_db, 0), states(h_s, n_db, 1),
            k_s, v_s.reshape(n_db, dec_seq, n_heads, 2 * head_dim))
```

```python
import functools
import math

import jax
import jax.numpy as jnp
from jax import lax
from jax.experimental import pallas as pl
from jax.experimental.pallas import tpu as pltpu

CHUNK = 64
CHUNK_SHIFT = CHUNK.bit_length() - 1
EPS = 1e-6
ROPE_THETA = 10000.0
LANES = 128
SUBLANES = 8
SSM_SEG = SUBLANES
S5_TRIP = 16
S5_AHEAD = 2
S5_SLOTS = 4
VMEM_LIMIT = 56 * 1024 * 1024

F32 = jnp.float32
BF16 = jnp.bfloat16
NEG = -1e30
LOG2_E = math.log2(math.e)


def _dot(a, b):
    return jnp.dot(a, b, preferred_element_type=F32)


def _dot_nt(a, b):
    return lax.dot_general(a, b, (((1,), (1,)), ((), ())), preferred_element_type=F32)


def _rms(x, g):
    return x * lax.rsqrt(jnp.mean(x * x, axis=-1, keepdims=True) + EPS) * g


def _params(*sem):
    return pltpu.CompilerParams(dimension_semantics=sem, vmem_limit_bytes=VMEM_LIMIT)


def _cmul(ar, ai, br, bi):
    return ar * br - ai * bi, ar * bi + ai * br


def _cpow(ar, ai, n):
    rr, ri = None, None
    br, bi = ar, ai
    while n:
        if n & 1:
            rr, ri = (br, bi) if rr is None else _cmul(rr, ri, br, bi)
        n >>= 1
        if n:
            br, bi = _cmul(br, bi, br, bi)
    return rr, ri


def _s5_kernel(*refs, chain, seg_len, chunks_per_seq, n_cb, cw):
    if chain:
        (x_ref, g_ref, p_ref, pt_ref, b_ref, c_ref, are_ref, aim_ref, d_ref,
         gy_ref, hre_ref, him_ref, u_scr, bu_scr, gyp_scr, cre_scr, cim_scr) = refs
    else:
        (x_ref, g_ref, p_ref, pt_ref, b_ref, c_ref, are_ref, aim_ref, d_ref,
         h0re_ref, h0im_ref, gy_ref, hre_ref, him_ref, u_scr, bu_scr, gyp_scr) = refs
    sw = cw // 2

    u = _rms(x_ref[...], g_ref[...]).astype(BF16)
    up = _dot(p_ref[...], u).astype(BF16)
    for cb in range(n_cb):
        u_scr[cb] = up[:, cb * LANES:(cb + 1) * LANES]

    if chain:
        @pl.when(pl.program_id(0) % chunks_per_seq == 0)
        def _():
            cre_scr[...] = jnp.zeros_like(cre_scr)
            cim_scr[...] = jnp.zeros_like(cim_scr)

    row_id = lax.broadcasted_iota(jnp.int32, (SSM_SEG, sw), 0)

    def project_in(cb, slot):
        bu_scr[slot] = _dot(u_scr[cb], b_ref[cb])

    def scan_and_project_out(cb, slot):
        h_scr = bu_scr.at[slot]
        are = are_ref[cb]
        aim = aim_ref[cb]
        if chain:
            hre = jnp.zeros((SSM_SEG, sw), F32)
            him = jnp.zeros((SSM_SEG, sw), F32)
        else:
            hre = h0re_ref[cb]
            him = h0im_ref[cb]
        for k in range(seg_len):
            r0 = k * SSM_SEG
            nre = are * hre - aim * him + h_scr[r0:r0 + SSM_SEG, 0:sw]
            nim = are * him + aim * hre + h_scr[r0:r0 + SSM_SEG, sw:cw]
            hre, him = nre, nim
            h_scr[r0:r0 + SSM_SEG, 0:sw] = hre
            h_scr[r0:r0 + SSM_SEG, sw:cw] = him

        if chain:
            pre, pim = _cpow(are[0:1], aim[0:1], seg_len)
            cur_re = cre_scr[cb][SSM_SEG - 1:SSM_SEG]
            cur_im = cim_scr[cb][SSM_SEG - 1:SSM_SEG]
            prev_re = jnp.zeros((SSM_SEG, sw), F32)
            prev_im = jnp.zeros((SSM_SEG, sw), F32)
            end_re = jnp.zeros((SSM_SEG, sw), F32)
            end_im = jnp.zeros((SSM_SEG, sw), F32)
            for j in range(SSM_SEG):
                prev_re = jnp.where(row_id == j, cur_re, prev_re)
                prev_im = jnp.where(row_id == j, cur_im, prev_im)
                mre, mim = _cmul(pre, pim, cur_re, cur_im)
                cur_re = mre + hre[j:j + 1]
                cur_im = mim + him[j:j + 1]
                end_re = jnp.where(row_id == j, cur_re, end_re)
                end_im = jnp.where(row_id == j, cur_im, end_im)
            cre_scr[cb] = end_re
            cim_scr[cb] = end_im
            hre_ref[cb] = end_re
            him_ref[cb] = end_im
            qre, qim = _cmul(are, aim, prev_re, prev_im)
            for k in range(seg_len):
                r0 = k * SSM_SEG
                h_scr[r0:r0 + SSM_SEG, 0:sw] = h_scr[r0:r0 + SSM_SEG, 0:sw] + qre
                h_scr[r0:r0 + SSM_SEG, sw:cw] = h_scr[r0:r0 + SSM_SEG, sw:cw] + qim
                if k + 1 < seg_len:
                    qre, qim = _cmul(are, aim, qre, qim)
        else:
            hre_ref[cb] = hre
            him_ref[cb] = him

        y = _dot(h_scr[...].astype(BF16), c_ref[cb])
        y = y + d_ref[cb] * u_scr[cb].astype(F32)
        gyp_scr[cb] = jax.nn.gelu(y, approximate=True).astype(BF16)

    def trip(t, carry):
        cb0 = t * S5_TRIP
        for c in range(min(S5_AHEAD, S5_TRIP)):
            project_in(cb0 + c, c % S5_SLOTS)
        for c in range(S5_TRIP):
            if c + S5_AHEAD < S5_TRIP:
                project_in(cb0 + c + S5_AHEAD, (c + S5_AHEAD) % S5_SLOTS)
            scan_and_project_out(cb0 + c, c % S5_SLOTS)
        return carry

    lax.fori_loop(0, n_cb // S5_TRIP, trip, 0)

    perm_t = pt_ref[...]
    for cb in range(n_cb):
        gy_ref[:, cb * LANES:(cb + 1) * LANES] = _dot(perm_t, gyp_scr[cb]).astype(BF16)


def _s5_call(x, g, ssm, h0, *, seg_len, chunks_per_seq):
    m_rows, d_model = x.shape
    chain = h0 is None
    t_rows = SSM_SEG * seg_len
    n_chunks = m_rows // t_rows
    n_cb, _, cw = ssm['b'].shape
    sw = cw // 2
    full3 = lambda i: (0, 0, 0)
    full2 = lambda i: (0, 0)
    in_specs = [
        pl.BlockSpec((t_rows, d_model), lambda i: (i, 0)),
        pl.BlockSpec((1, d_model), full2),
        pl.BlockSpec((t_rows, t_rows), full2),
        pl.BlockSpec((t_rows, t_rows), full2),
        pl.BlockSpec((n_cb, LANES, cw), full3),
        pl.BlockSpec((n_cb, cw, LANES), full3),
        pl.BlockSpec((n_cb, SSM_SEG, sw), full3),
        pl.BlockSpec((n_cb, SSM_SEG, sw), full3),
        pl.BlockSpec((n_cb, 1, LANES), full3),
    ]
    args = [x, g, ssm['perm'], ssm['perm_t'], ssm['b'], ssm['c'], ssm['a_re'], ssm['a_im'], ssm['d']]
    if not chain:
        in_specs += [pl.BlockSpec((n_cb, SSM_SEG, sw), lambda i: (0, i, 0))] * 2
        args += [h0[0], h0[1]]
    scratch = [
        pltpu.VMEM((n_cb, t_rows, LANES), BF16),
        pltpu.VMEM((S5_SLOTS, t_rows, cw), F32),
        pltpu.VMEM((n_cb, t_rows, LANES), BF16),
    ]
    if chain:
        scratch += [pltpu.VMEM((n_cb, SSM_SEG, sw), F32)] * 2
    state_shape = jax.ShapeDtypeStruct((n_cb, n_chunks * SSM_SEG, sw), F32)
    return pl.pallas_call(
        functools.partial(_s5_kernel, chain=chain, seg_len=seg_len, chunks_per_seq=chunks_per_seq,
                          n_cb=n_cb, cw=cw),
        grid=(n_chunks,),
        in_specs=in_specs,
        out_specs=[
            pl.BlockSpec((t_rows, d_model), lambda i: (i, 0)),
            pl.BlockSpec((n_cb, SSM_SEG, sw), lambda i: (0, i, 0)),
            pl.BlockSpec((n_cb, SSM_SEG, sw), lambda i: (0, i, 0)),
        ],
        out_shape=[jax.ShapeDtypeStruct((m_rows, d_model), BF16), state_shape, state_shape],
        scratch_shapes=scratch,
        compiler_params=_params("arbitrary"),
        name="s5_chain" if chain else "s5_step",
    )(*args)


def _ssm_tables(lam_re, lam_im, log_step, b_re, b_im, c_re, c_im, d, seg_len):
    n_groups, n_state, n_chan = b_re.shape
    gpb = LANES // n_chan
    n_cb = n_groups // gpb
    sw = gpb * n_state
    lam_re = lam_re.astype(F32)
    lam_im = lam_im.astype(F32)
    dt = jnp.exp(log_step.astype(F32))[:, None]
    mag = jnp.exp(lam_re * dt)
    a_re = mag * jnp.cos(lam_im * dt)
    a_im = mag * jnp.sin(lam_im * dt)
    den = lam_re * lam_re + lam_im * lam_im
    q_re = ((a_re - 1.0) * lam_re + a_im * lam_im) / den
    q_im = (a_im * lam_re - (a_re - 1.0) * lam_im) / den
    bb_re = q_re[..., None] * b_re.astype(F32) - q_im[..., None] * b_im.astype(F32)
    bb_im = q_re[..., None] * b_im.astype(F32) + q_im[..., None] * b_re.astype(F32)
    eye = jnp.eye(gpb, dtype=F32)

    def bmat(bb):
        t = bb.reshape(n_cb, gpb, n_state, n_chan)
        t = jnp.einsum('cgnp,gh->cgphn', t, eye)
        return t.reshape(n_cb, gpb * n_chan, sw)

    def cmat(cc):
        t = cc.astype(F32).reshape(n_cb, gpb, n_chan, n_state)
        t = jnp.einsum('cgpn,gh->cgnhp', t, eye)
        return t.reshape(n_cb, sw, gpb * n_chan)

    b_full = jnp.concatenate([bmat(bb_re), bmat(bb_im)], axis=-1).astype(BF16)
    c_full = jnp.concatenate([cmat(c_re), -cmat(c_im)], axis=1).astype(BF16)
    rep = lambda a: jnp.broadcast_to(a.reshape(n_cb, 1, sw), (n_cb, SSM_SEG, sw))
    t_rows = SSM_SEG * seg_len
    r = jnp.arange(t_rows)
    src = (r % SSM_SEG) * seg_len + r // SSM_SEG
    perm = (src[:, None] == jnp.arange(t_rows)[None, :]).astype(BF16)
    return dict(b=b_full, c=c_full, a_re=rep(a_re), a_im=rep(a_im),
                d=d.astype(F32).reshape(n_cb, 1, LANES), perm=perm, perm_t=perm.T)


def _glu_kernel(a_ref, wv_ref, wg_ref, x_ref, o_ref):
    a = a_ref[...]
    zv = _dot(a, wv_ref[...])
    zg = _dot(a, wg_ref[...])
    o_ref[...] = x_ref[...] + zv * jax.nn.sigmoid(zg)


def _glu_call(a, w, x, *, tm, tn):
    m_rows, d_in = a.shape
    d_out = x.shape[1]
    nj = d_out // tn
    return pl.pallas_call(
        _glu_kernel,
        grid=(m_rows // tm, nj),
        in_specs=[
            pl.BlockSpec((tm, d_in), lambda i, j: (i, 0)),
            pl.BlockSpec((d_in, tn), lambda i, j: (0, j)),
            pl.BlockSpec((d_in, tn), lambda i, j: (0, j + nj)),
            pl.BlockSpec((tm, tn), lambda i, j: (i, j)),
        ],
        out_specs=pl.BlockSpec((tm, tn), lambda i, j: (i, j)),
        out_shape=jax.ShapeDtypeStruct((m_rows, d_out), F32),
        compiler_params=_params("parallel", "arbitrary"),
        name="glu",
    )(a, w, w, x)


def _ffn_kernel(x_ref, g_ref, wa_ref, wb_ref, wo_ref, o_ref, xn_scr):
    @pl.when(pl.program_id(1) == 0)
    def _():
        x = x_ref[...]
        xn_scr[...] = _rms(x, g_ref[...]).astype(BF16)
        o_ref[...] = x

    xn = xn_scr[...]
    a = _dot(xn, wa_ref[...])
    b = _dot(xn, wb_ref[...])
    h = (a * jax.nn.sigmoid(a) * b).astype(BF16)
    o_ref[...] += _dot(h, wo_ref[...])


def _ffn_call(x, g, w_in, w_out, *, tm, tf):
    m_rows, d_model = x.shape
    d_ff = w_out.shape[0]
    nj = d_ff // tf
    return pl.pallas_call(
        _ffn_kernel,
        grid=(m_rows // tm, nj),
        in_specs=[
            pl.BlockSpec((tm, d_model), lambda i, j: (i, 0)),
            pl.BlockSpec((1, d_model), lambda i, j: (0, 0)),
            pl.BlockSpec((d_model, tf), lambda i, j: (0, j)),
            pl.BlockSpec((d_model, tf), lambda i, j: (0, j + nj)),
            pl.BlockSpec((tf, d_model), lambda i, j: (j, 0)),
        ],
        out_specs=pl.BlockSpec((tm, d_model), lambda i, j: (i, 0)),
        out_shape=jax.ShapeDtypeStruct((m_rows, d_model), F32),
        scratch_shapes=[pltpu.VMEM((tm, d_model), BF16)],
        compiler_params=_params("parallel", "arbitrary"),
        name="ffn",
    )(x, g, w_in, w_in, w_out)


def _proj_kernel(*refs, head_norm, scale, outs, head_dim, tn):
    n_out = len(outs)
    if head_norm:
        x_ref, g_ref, w_ref, hn_ref, cos_ref, sin_ref = refs[:6]
        out_refs = refs[6:6 + n_out]
    else:
        x_ref, g_ref, w_ref = refs[:3]
        out_refs = refs[3:3 + n_out]
    xn = _rms(x_ref[...], g_ref[...]).astype(BF16)
    tm = xn.shape[0]
    n_tiles = w_ref.shape[1] // tn
    if head_norm:
        half = head_dim // 2
        hd_shift = head_dim.bit_length() - 1
        gi = lax.broadcasted_iota(jnp.int32, (LANES, LANES), 0) >> hd_shift
        gj = lax.broadcasted_iota(jnp.int32, (LANES, LANES), 1) >> hd_shift
        gmat = (gi == gj).astype(BF16)
        lane = lax.broadcasted_iota(jnp.int32, (tm, LANES), 1)
        first_half = (lane & (head_dim - 1)) < half
        cos = cos_ref[...]
        sin = sin_ref[...]
        hn = hn_ref[...]
    need_t = any(kind.startswith("cols") for kind in outs)

    def matmul(t):
        return _dot(xn, w_ref[:, t * tn:(t + 1) * tn])

    def finish(t, y):
        for s in range(tn // LANES):
            cs = slice(t * tn + s * LANES, t * tn + (s + 1) * LANES)
            r = y[:, s * LANES:(s + 1) * LANES]
            if head_norm:
                ss = _dot((r * r).astype(BF16), gmat)
                yn = r * lax.rsqrt(ss * (1.0 / head_dim) + EPS) * hn
                partner = jnp.where(first_half, pltpu.roll(yn, LANES - half, 1), pltpu.roll(yn, half, 1))
                r = yn * cos + partner * sin
                if scale != 1.0:
                    r = r * scale
            rt = r.T if need_t else None
            for kind, o_ref in zip(outs, out_refs):
                if kind == "cols_f32":
                    o_ref[cs, :] = rt
                elif kind == "cols_bf16":
                    o_ref[0, 0, cs, :] = rt.astype(BF16)
                else:
                    o_ref[:, cs] = r.astype(o_ref.dtype)

    pending = matmul(0)
    for t in range(1, n_tiles):
        nxt = matmul(t)
        finish(t - 1, pending)
        pending = nxt
    finish(n_tiles - 1, pending)


def _proj_call(x, g, w, col0, n_cols, outs, *, tm, tn, rope=None, scale=1.0, head_dim=64, name="proj", seq=None):
    m_rows, d_model = x.shape
    head_norm = rope is not None
    out_specs, out_shape = [], []
    for kind in outs:
        if kind == "cols_f32":
            tiles = seq // tm
            out_specs.append(pl.BlockSpec((n_cols, tm), lambda i: (i // tiles, i % tiles)))
            out_shape.append(jax.ShapeDtypeStruct((m_rows // seq * n_cols, seq), F32))
        elif kind == "cols_bf16":
            tiles = seq // tm
            out_specs.append(pl.BlockSpec((1, 1, n_cols, tm), lambda i: (i // tiles, i % tiles, 0, 0)))
            out_shape.append(jax.ShapeDtypeStruct((m_rows // seq, tiles, n_cols, tm), BF16))
        else:
            out_specs.append(pl.BlockSpec((tm, n_cols), lambda i: (i, 0)))
            out_shape.append(jax.ShapeDtypeStruct((m_rows, n_cols), F32 if kind == "rows_f32" else BF16))
    in_specs = [
        pl.BlockSpec((tm, d_model), lambda i: (i, 0)),
        pl.BlockSpec((1, d_model), lambda i: (0, 0)),
        pl.BlockSpec((d_model, n_cols), lambda i: (0, col0 // n_cols)),
    ]
    args = [x, g, w]
    if head_norm:
        hn, cos, sin = rope
        nrep = cos.shape[0] // tm
        in_specs += [
            pl.BlockSpec((1, LANES), lambda i: (0, 0)),
            pl.BlockSpec((tm, LANES), lambda i: (i % nrep, 0)),
            pl.BlockSpec((tm, LANES), lambda i: (i % nrep, 0)),
        ]
        args += [hn, cos, sin]
    return pl.pallas_call(
        functools.partial(_proj_kernel, head_norm=head_norm, scale=scale, outs=tuple(outs), head_dim=head_dim,
                          tn=tn),
        grid=(m_rows // tm,),
        in_specs=in_specs,
        out_specs=out_specs,
        out_shape=out_shape,
        compiler_params=_params("parallel"),
        name=name,
    )(*args)


def _resmm_kernel(a_ref, w_ref, x_ref, o_ref):
    o_ref[...] = x_ref[...] + _dot(a_ref[...], w_ref[...])


def _resmm_call(a, w, x, *, tm, tn):
    m_rows, d_in = a.shape
    d_out = w.shape[1]
    return pl.pallas_call(
        _resmm_kernel,
        grid=(m_rows // tm, d_out // tn),
        in_specs=[
            pl.BlockSpec((tm, d_in), lambda i, j: (i, 0)),
            pl.BlockSpec((d_in, tn), lambda i, j: (0, j)),
            pl.BlockSpec((tm, tn), lambda i, j: (i, j)),
        ],
        out_specs=pl.BlockSpec((tm, tn), lambda i, j: (i, j)),
        out_shape=jax.ShapeDtypeStruct((m_rows, d_out), F32),
        compiler_params=_params("parallel", "arbitrary"),
        name="out_proj",
    )(a, w, x)


def _stack_maps(q, head_dim):
    lane = lax.broadcasted_iota(jnp.int32, q.shape, 1)
    zero = jnp.zeros_like(q)
    return jnp.concatenate([jnp.where(lane < head_dim, q, zero), jnp.where(lane >= head_dim, q, zero)], axis=0)


ONES_ROWS = 16


def _attn_prompt_kernel(lam_ref, sub_ref, qt_ref, k_ref, vt_ref, o_ref, acc_scr, st_scr, m_scr, qq_scr,
                        *, tq, head_dim, out_scale):
    hw = 2 * head_dim
    nq = qt_ref.shape[1]
    ones = jnp.ones((ONES_ROWS, tq), BF16)

    def stage_queries(qi):
        qt = qt_ref[0, qi]
        feat = lax.broadcasted_iota(jnp.int32, qt.shape, 0)
        zero = jnp.zeros_like(qt)
        qq_scr[...] = jnp.concatenate(
            [jnp.where(feat < head_dim, qt, zero), jnp.where(feat >= head_dim, qt, zero)], axis=1)

    def scores(kj, slot):
        start = pl.multiple_of(kj * tq, tq)
        st_scr[slot] = _dot(k_ref[pl.ds(start, tq), :], qq_scr[...])

    def softmax_values(kj, slot, masked):
        st = st_scr[slot]
        if masked:
            kpos = lax.broadcasted_iota(jnp.int32, st.shape, 0)
            qpos = lax.broadcasted_iota(jnp.int32, st.shape, 1)
            qpos = jnp.where(qpos >= tq, qpos - tq, qpos)
            st = jnp.where((kpos >> CHUNK_SHIFT) <= (qpos >> CHUNK_SHIFT), st, NEG)
        m_prev = m_scr[...]
        m_new = jnp.maximum(m_prev, jnp.max(st, axis=0, keepdims=True))
        m_scr[...] = m_new
        alpha = jnp.exp2(m_prev - m_new)
        pt = jnp.exp2(st - m_new).astype(BF16)
        vt1 = jnp.concatenate([vt_ref[0, kj], ones], axis=0)
        acc_scr[...] = alpha * acc_scr[...] + _dot(vt1, pt)

    def two_tiles(t, carry):
        scores(2 * t + 1, 1)
        softmax_values(2 * t, 0, False)
        scores(2 * t + 2, 0)
        softmax_values(2 * t + 1, 1, False)
        return carry

    def query_tile(qi, carry):
        acc_scr[...] = jnp.zeros_like(acc_scr)
        m_scr[...] = jnp.full_like(m_scr, NEG)
        lax.fori_loop(0, qi // 2, two_tiles, 0)

        @pl.when(qi % 2 == 0)
        def _():
            softmax_values(qi, 0, True)

        @pl.when(qi % 2 == 1)
        def _():
            scores(qi, 1)
            softmax_values(qi - 1, 0, False)
            softmax_values(qi, 1, True)

        stage_queries(jnp.minimum(qi + 1, nq - 1))
        scores(0, 0)
        acc = acc_scr[...]
        ot = acc[:hw] * (1.0 / acc[hw:hw + 1])
        ot = ot[:, :tq] - lam_ref[...] * ot[:, tq:]
        ot = ot * lax.rsqrt(jnp.mean(ot * ot, axis=0, keepdims=True) + EPS) * (sub_ref[...] * out_scale)
        o_ref[pl.ds(pl.multiple_of(qi * tq, tq), tq), :] = ot.T.astype(o_ref.dtype)
        return carry

    stage_queries(0)
    scores(0, 0)
    lax.fori_loop(0, nq, query_tile, 0)


def _attn_prompt_call(lam, sub, qt, k, vt, *, n_batch, seq, n_heads, head_dim, out_scale):
    tq = qt.shape[3]
    nq = seq // tq
    hw = 2 * head_dim
    return pl.pallas_call(
        functools.partial(_attn_prompt_kernel, tq=tq, head_dim=head_dim, out_scale=out_scale),
        grid=(n_batch, n_heads),
        in_specs=[
            pl.BlockSpec((1, tq), lambda b, h: (0, 0)),
            pl.BlockSpec((hw, 1), lambda b, h: (0, 0)),
            pl.BlockSpec((1, nq, hw, tq), lambda b, h: (b, 0, h, 0)),
            pl.BlockSpec((seq, hw), lambda b, h: (b, h)),
            pl.BlockSpec((1, nq, hw, tq), lambda b, h: (b, 0, h, 0)),
        ],
        out_specs=pl.BlockSpec((seq, hw), lambda b, h: (b, h)),
        out_shape=jax.ShapeDtypeStruct(k.shape, BF16),
        scratch_shapes=[pltpu.VMEM((hw + ONES_ROWS, 2 * tq), F32), pltpu.VMEM((2, tq, 2 * tq), F32),
                        pltpu.VMEM((1, 2 * tq), F32), pltpu.VMEM((hw, 2 * tq), BF16)],
        compiler_params=_params("parallel", "parallel"),
        name="attn_prompt",
    )(lam, sub, qt, k, vt)


def _attn_sample_kernel(lam_ref, sub_ref, q_ref, ck_ref, cv_ref, nk_ref, nv_ref, o_ref, qq_scr, s_scr, m_scr,
                        acc_scr, *, n_heads, head_dim, n_cache_tiles, past_len, out_scale):
    j = pl.program_id(1)
    hw = 2 * head_dim
    rows = q_ref.shape[0]
    hr = 2 * rows

    @pl.when(j == 0)
    def _():
        m_scr[...] = jnp.full_like(m_scr, NEG)
        acc_scr[...] = jnp.zeros_like(acc_scr)
        for h in range(n_heads):
            qq_scr[h] = _stack_maps(q_ref[:, h * hw:(h + 1) * hw], head_dim)

    def softmax_and_values(n_keys, v_of_head):
        m_prev = m_scr[...]
        s_tiles = [s_scr[:, t * LANES:(t + 1) * LANES] for t in range(max(n_keys // LANES, 1))]
        if n_keys < LANES:
            m_cur = jnp.max(s_scr[:, 0:n_keys], axis=-1, keepdims=True)
        else:
            m_cur = functools.reduce(jnp.maximum, s_tiles).max(axis=-1, keepdims=True)
        m_new = jnp.maximum(m_prev, m_cur)
        m_scr[...] = m_new
        alpha = jnp.exp2(m_prev - m_new)
        if n_keys < LANES:
            p = jnp.exp2(s_scr[:, 0:n_keys] - m_new[:, 0:n_keys]).astype(BF16)
        else:
            p = jnp.concatenate([jnp.exp2(st - m_new) for st in s_tiles], axis=1).astype(BF16)
        ones = jnp.ones((n_keys, hw), BF16)
        for h in range(n_heads):
            rs = slice(h * hr, (h + 1) * hr)
            v1 = jnp.concatenate([v_of_head(h), ones], axis=1)
            a = alpha[rs]
            acc_scr[h] = jnp.concatenate([a, a], axis=1) * acc_scr[h] + _dot(p[rs], v1)

    @pl.when(j < n_cache_tiles)
    def _():
        for h in range(n_heads):
            s_scr[h * hr:(h + 1) * hr, :] = _dot(qq_scr[h], ck_ref[h * hw:(h + 1) * hw, :].astype(BF16))
        n_keys = ck_ref.shape[1]
        softmax_and_values(n_keys, lambda h: cv_ref[pl.ds(h, n_keys, stride=n_heads), :].astype(BF16))

    @pl.when(j == n_cache_tiles)
    def _():
        n_new = nk_ref.shape[0]
        row = lax.broadcasted_iota(jnp.int32, (hr, n_new), 0)
        row = jnp.where(row >= rows, row - rows, row)
        col = lax.broadcasted_iota(jnp.int32, (hr, n_new), 1)
        visible = ((past_len + col) >> CHUNK_SHIFT) <= ((past_len + row) >> CHUNK_SHIFT)
        for h in range(n_heads):
            s = _dot_nt(qq_scr[h], nk_ref[:, h * hw:(h + 1) * hw])
            s_scr[h * hr:(h + 1) * hr, 0:n_new] = jnp.where(visible, s, NEG)
        softmax_and_values(n_new, lambda h: nv_ref[:, h * hw:(h + 1) * hw])
        lam = lam_ref[...]
        sub = sub_ref[...]
        for h in range(n_heads):
            acc = acc_scr[h]
            o = acc[:, :hw] * (1.0 / acc[:, hw:])
            o = o[:rows] - lam * o[rows:]
            o_ref[:, h * hw:(h + 1) * hw] = (_rms(o, sub) * out_scale).astype(o_ref.dtype)


def _attn_sample_call(lam, sub, q, cache_k, cache_v, new_k, new_v, *, n_batch, dec_seq, past_len, n_heads,
                      head_dim, tk, out_scale):
    d_model = q.shape[1]
    hw = 2 * head_dim
    n_tiles = past_len // tk
    last = n_tiles - 1
    cache_map = lambda b, j: (b * n_tiles + jnp.minimum(j, last), 0)
    cache_kt_map = lambda b, j: (b, jnp.minimum(j, last))
    return pl.pallas_call(
        functools.partial(_attn_sample_kernel, n_heads=n_heads, head_dim=head_dim, n_cache_tiles=n_tiles,
                          past_len=past_len, out_scale=out_scale),
        grid=(n_batch, n_tiles + 1),
        in_specs=[
            pl.BlockSpec((1, LANES), lambda b, j: (0, 0)),
            pl.BlockSpec((1, hw), lambda b, j: (0, 0)),
            pl.BlockSpec((dec_seq, d_model), lambda b, j: (b, 0)),
            pl.BlockSpec((d_model, tk), cache_kt_map),
            pl.BlockSpec((tk * n_heads, hw), cache_map),
            pl.BlockSpec((dec_seq, d_model), lambda b, j: (b, 0)),
            pl.BlockSpec((dec_seq, d_model), lambda b, j: (b, 0)),
        ],
        out_specs=pl.BlockSpec((dec_seq, d_model), lambda b, j: (b, 0)),
        out_shape=jax.ShapeDtypeStruct(q.shape, BF16),
        scratch_shapes=[pltpu.VMEM((n_heads, 2 * dec_seq, hw), BF16),
                        pltpu.VMEM((n_heads * 2 * dec_seq, tk), F32),
                        pltpu.VMEM((n_heads * 2 * dec_seq, LANES), F32),
                        pltpu.VMEM((n_heads, 2 * dec_seq, 2 * hw), F32)],
        compiler_params=_params("parallel", "arbitrary"),
        name="attn_sample",
    )(lam, sub, q, cache_k, cache_v, new_k, new_v)


def _rope_tables(pos, head_dim, reps):
    half = head_dim // 2
    inv = ROPE_THETA ** (-jnp.arange(half, dtype=F32) / half)
    ang = pos.astype(F32)[:, None] * inv[None, :]
    cos = jnp.cos(ang)
    sin = jnp.sin(ang)
    n_groups = LANES // head_dim
    cos_t = jnp.tile(cos, (reps, 2 * n_groups))
    sin_t = jnp.tile(jnp.concatenate([-sin, sin], axis=-1), (reps, n_groups))
    return cos_t, sin_t


def _row_tile(m_rows, cap):
    t = cap
    while m_rows % t:
        t //= 2
    return t


def _trunk(x, pos, n_seq, h0, cache, p, prompt):
    m_rows, d_model = x.shape
    slen = m_rows // n_seq
    n_a = p['norm_ssm'].shape[0]
    depth = n_a + p['norm_attn'].shape[0]
    head_dim = p['k_norm'].shape[0]
    hw = 2 * head_dim
    n_heads = d_model // hw
    k_cols = n_heads * hw
    tm = _row_tile(m_rows, 512)
    tm_big = _row_tile(m_rows, 1024)
    d_ff = p['w_ffn_out'][0].shape[0]
    tf = 512 if d_ff % 512 == 0 else 256
    row = lambda a: a.astype(F32).reshape(1, -1)

    if prompt:
        seg_len = 32
        rope_reps = 1
    else:
        seg_len = slen
        rope_reps = tm // slen
    cos_t, sin_t = _rope_tables(pos, head_dim, rope_reps)

    new_h = []
    k_f32 = v_f32 = k_b = v_b = None
    for layer in range(depth):
        if layer < n_a:
            i = layer
            tabs = _ssm_tables(p['ssm_lam_re'][i], p['ssm_lam_im'][i], p['ssm_log_step'][i], p['ssm_b_re'][i],
                               p['ssm_b_im'][i], p['ssm_c_re'][i], p['ssm_c_im'][i], p['ssm_d'][i], seg_len)
            n_cb, _, sw = tabs['a_re'].shape
            if prompt:
                t_rows = SSM_SEG * seg_len
                gy, h_re, h_im = _s5_call(x, row(p['norm_ssm'][i]), tabs, None, seg_len=seg_len,
                                          chunks_per_seq=slen // t_rows)
                last = (jnp.arange(n_seq) + 1) * (slen // seg_len) - 1
                h_re, h_im = h_re[:, last], h_im[:, last]
            else:
                to_cb = lambda a: a.astype(F32).reshape(n_seq, n_cb, sw).transpose(1, 0, 2)
                gy, h_re, h_im = _s5_call(x, row(p['norm_ssm'][i]), tabs, (to_cb(h0[0][i]), to_cb(h0[1][i])),
                                          seg_len=seg_len, chunks_per_seq=1)
            from_cb = lambda a: a.transpose(1, 0, 2).reshape(n_seq, -1)
            new_h.append((from_cb(h_re), from_cb(h_im)))
            x = _glu_call(gy, p['w_glu'][i], x, tm=tm_big, tn=1024)
        else:
            if layer == n_a:
                g_kv = row(p['norm_kv'])
                hn_k = jnp.tile(row(p['k_norm']), (1, LANES // head_dim))
                k_f32, k_b = _proj_call(x, g_kv, p['w_kv'], 0, k_cols,
                                        ("cols_f32" if prompt else "rows_f32", "rows_bf16"), tm=tm, tn=512,
                                        rope=(hn_k, cos_t, sin_t), head_dim=head_dim, name="k_proj", seq=slen)
                if prompt:
                    k_f32 = k_f32.reshape(n_seq, n_heads, 2, head_dim, slen).transpose(0, 4, 1, 2, 3)
                else:
                    k_f32 = k_f32.reshape(n_seq, slen, n_heads, 2, head_dim)
                v_f32, v_b = _proj_call(x, g_kv, p['w_kv'], k_cols, p['w_kv'].shape[1] - k_cols,
                                        ("rows_f32", "cols_bf16" if prompt else "rows_bf16"), tm=tm, tn=512,
                                        name="v_proj", seq=slen)
            j = layer - n_a
            lambda_init = 0.8 - 0.6 * math.exp(-0.3 * layer)
            hn_q = jnp.tile(row(p['q_norm'][j]), (1, LANES // head_dim))
            (q,) = _proj_call(x, row(p['norm_attn'][j]), p['w_q'][j], 0, k_cols,
                              ("cols_bf16" if prompt else "rows_bf16",), tm=tm, tn=512,
                              rope=(hn_q, cos_t, sin_t), scale=head_dim ** -0.5 * LOG2_E, head_dim=head_dim,
                              name="q_proj", seq=slen)
            lam = (jnp.exp(jnp.sum(p['lambda_q1'][j].astype(F32) * p['lambda_k1'][j].astype(F32)))
                   - jnp.exp(jnp.sum(p['lambda_q2'][j].astype(F32) * p['lambda_k2'][j].astype(F32)))
                   + lambda_init)
            sub = row(p['subln'][j])
            if prompt:
                o = _attn_prompt_call(jnp.full((1, tm), lam, F32), sub.reshape(-1, 1), q, k_b, v_b, n_batch=n_seq,
                                      seq=slen, n_heads=n_heads, head_dim=head_dim, out_scale=1.0 - lambda_init)
            else:
                lam = jnp.full((1, LANES), lam, F32)
                cache_k, cache_v = cache
                past_len = cache_k.shape[1]
                cache_kt = cache_k.transpose(0, 2, 3, 4, 1).reshape(n_seq * d_model, past_len)
                o = _attn_sample_call(lam, sub, q, cache_kt,
                                      cache_v.reshape(n_seq * past_len * n_heads, hw), k_b, v_b, n_batch=n_seq,
                                      dec_seq=slen, past_len=past_len, n_heads=n_heads, head_dim=head_dim,
                                      tk=512, out_scale=1.0 - lambda_init)
            x = _resmm_call(o, p['w_o'][j], x, tm=tm_big, tn=1024)
        x = _ffn_call(x, row(p['norm_ffn'][layer]), p['w_ffn_in'][layer], p['w_ffn_out'][layer], tm=tm, tf=tf)
    return x, new_h, k_f32, v_f32


def kernel(x_prompt, x_sample, state_ssm_re, state_ssm_im, cache_k, cache_v, norm_ssm, ssm_lam_re, ssm_lam_im, ssm_log_step, ssm_b_re, ssm_b_im, ssm_c_re, ssm_c_im, ssm_d, w_glu, norm_kv, w_kv, k_norm, norm_attn, w_q, q_norm, lambda_q1, lambda_k1, lambda_q2, lambda_k2, subln, w_o, norm_ffn, w_ffn_in, w_ffn_out):
    n_b, seq, d_model = x_prompt.shape
    n_db, dec_seq, _ = x_sample.shape
    past_len = cache_k.shape[1]
    head_dim = k_norm.shape[0]
    n_heads = d_model // (2 * head_dim)
    n_groups, n_state = ssm_lam_re.shape[1:]
    assert ssm_b_re.shape[3] * n_groups == d_model and LANES % ssm_b_re.shape[3] == 0
    assert 2 * head_dim == LANES and seq % (SSM_SEG * 32) == 0 and n_db % SSM_SEG == 0
    assert past_len % CHUNK == 0 and dec_seq % 16 == 0

    p = dict(norm_ssm=norm_ssm, ssm_lam_re=ssm_lam_re, ssm_lam_im=ssm_lam_im, ssm_log_step=ssm_log_step,
             ssm_b_re=ssm_b_re, ssm_b_im=ssm_b_im, ssm_c_re=ssm_c_re, ssm_c_im=ssm_c_im, ssm_d=ssm_d,
             w_glu=w_glu.astype(BF16), norm_kv=norm_kv, w_kv=w_kv.astype(BF16), k_norm=k_norm,
             norm_attn=norm_attn, w_q=w_q.astype(BF16), q_norm=q_norm, lambda_q1=lambda_q1,
             lambda_k1=lambda_k1, lambda_q2=lambda_q2, lambda_k2=lambda_k2, subln=subln,
             w_o=w_o.astype(BF16), norm_ffn=norm_ffn, w_ffn_in=w_ffn_in.astype(BF16),
             w_ffn_out=w_ffn_out.astype(BF16))

    pos_p = jnp.arange(seq, dtype=jnp.int32)
    y_p, h_p, k_p, v_p = _trunk(x_prompt.reshape(n_b * seq, d_model), pos_p, n_b, None, None, p, True)
    pos_s = past_len + jnp.arange(dec_seq, dtype=jnp.int32)
    y_s, h_s, k_s, v_s = _trunk(x_sample.reshape(n_db * dec_seq, d_model), pos_s, n_db,
                                (state_ssm_re, state_ssm_im), (cache_k, cache_v), p, False)

    def states(hs, n_seq, part):
        return jnp.stack([h[part].reshape(n_seq, n_groups, n_state) for h in hs])

    return (y_p.reshape(n_b, seq, d_model), y_s.reshape(n_db, dec_seq, d_model),
            states(h_p, n_b, 0), states(h_p, n_b, 1),
            k_p, v_p.reshape(n_b, seq, n_heads, 2 * head_dim),
            states(h_s, n_db, 0), states(h_s, n_db, 1),
            k_s, v_s.reshape(n_db, dec_seq, n_heads, 2 * head_dim))
```
